```python
import math
import jax, jax.numpy as jnp
from jax import lax
import numpy as np


D_MODEL = 1024
BATCH = 8
SEQ = 8192
DEPTH = 4

N_MIXERS = 2
HEAD_DIM = 64
A_Q_HEADS = D_MODEL // HEAD_DIM
A_KV_HEADS = A_Q_HEADS // 4
A_GROUP = A_Q_HEADS // A_KV_HEADS
WINDOW = 128
B_HEADS = D_MODEL // HEAD_DIM
BLOCK = 128
REL_BUCKETS = 32
REL_MAX_DIST = 128
D_FF = 2816
EPS = 1e-6
FORGET_BIAS_INIT = 2.0
N_A_LAYERS = (DEPTH + 1) // 2
N_B_LAYERS = DEPTH // 2
A_IN = (A_Q_HEADS + 2 * A_KV_HEADS) * HEAD_DIM
B_IN = 3 * B_HEADS * HEAD_DIM + B_HEADS

kernel_name = 'hybrid_swa_sink_fox_macaron_adaln'


def rmsnorm(x, g):
    xf = x.astype(jnp.float32)
    y = xf * lax.rsqrt(jnp.mean(xf * xf, axis=-1, keepdims=True) + EPS) * g.astype(jnp.float32)
    return y.astype(x.dtype)


def modulate(x, g, shift, scale):
    return rmsnorm(x, g) * (1 + scale[:, None, :]) + shift[:, None, :]


def swiglu(h, w13, w2):
    gate, up = jnp.split(h @ w13, 2, axis=-1)
    return (jax.nn.silu(gate) * up) @ w2


def rel_bucket(dist):
    n = jnp.maximum(dist, 0)
    max_exact = REL_BUCKETS // 2
    nf = jnp.maximum(n, 1).astype(jnp.float32)
    large = max_exact + (jnp.log(nf / max_exact) / math.log(REL_MAX_DIST / max_exact)
                         * (REL_BUCKETS - max_exact)).astype(jnp.int32)
    large = jnp.minimum(large, REL_BUCKETS - 1)
    return jnp.where(n < max_exact, n, large)


def swa_mixer(h, w_in, w_out, q_g, k_g, sink, rel_bias):
    B, S, _ = h.shape
    nblk = S // BLOCK
    proj = h @ w_in
    q, k, v = jnp.split(proj, [A_Q_HEADS * HEAD_DIM, (A_Q_HEADS + A_KV_HEADS) * HEAD_DIM], axis=-1)
    q = rmsnorm(q.reshape(B, S, A_KV_HEADS, A_GROUP, HEAD_DIM), q_g) * (HEAD_DIM ** -0.5)
    k = rmsnorm(k.reshape(B, S, A_KV_HEADS, HEAD_DIM), k_g)
    v = v.reshape(B, S, A_KV_HEADS, HEAD_DIM)
    pad = ((0, 0), (BLOCK, 0), (0, 0), (0, 0))
    kp = jnp.pad(k, pad)
    vp = jnp.pad(v, pad)
    qi = jnp.arange(BLOCK)[:, None] + BLOCK
    kj = jnp.arange(2 * BLOCK)[None, :]
    dist = qi - kj
    band = (dist >= 0) & (dist < WINDOW)
    bias = jnp.transpose(rel_bias[rel_bucket(dist)], (2, 0, 1)).astype(jnp.float32)
    bias = bias.reshape(A_KV_HEADS, A_GROUP, BLOCK, 2 * BLOCK)
    sink_f = sink.astype(jnp.float32).reshape(A_KV_HEADS, A_GROUP)[None, :, :, None]

    def block(i):
        start = i * BLOCK
        qb = lax.dynamic_slice_in_dim(q, start, BLOCK, axis=1)
        kb = lax.dynamic_slice_in_dim(kp, start, 2 * BLOCK, axis=1)
        vb = lax.dynamic_slice_in_dim(vp, start, 2 * BLOCK, axis=1)
        s = jnp.einsum('bqhgd,bkhd->bhgqk', qb, kb).astype(jnp.float32) + bias
        valid = band & (start - BLOCK + kj >= 0)
        s = jnp.where(valid, s, -jnp.inf)
        m = jnp.maximum(jnp.max(s, axis=-1), sink_f)
        p = jnp.exp(s - m[..., None])
        denom = jnp.sum(p, axis=-1) + jnp.exp(sink_f - m)
        p = p / denom[..., None]
        return jnp.einsum('bhgqk,bkhd->bqhgd', p.astype(vb.dtype), vb)

    out = lax.map(block, jnp.arange(nblk))
    out = jnp.moveaxis(out, 0, 1).reshape(B, S, A_Q_HEADS * HEAD_DIM)
    return out @ w_out


def fox_mixer(h, w_in, w_out, b_f, q_g, k_g):
    B, S, _ = h.shape
    nblk = S // BLOCK
    HD = B_HEADS * HEAD_DIM
    proj = h @ w_in
    q, k, v, fl = jnp.split(proj, [HD, 2 * HD, 3 * HD], axis=-1)
    q = rmsnorm(q.reshape(B, S, B_HEADS, HEAD_DIM), q_g) * (HEAD_DIM ** -0.5)
    k = rmsnorm(k.reshape(B, S, B_HEADS, HEAD_DIM), k_g)
    v = v.reshape(B, S, B_HEADS, HEAD_DIM)
    log_f = jax.nn.log_sigmoid(fl.astype(jnp.float32) + b_f.astype(jnp.float32))
    F = jnp.transpose(jnp.cumsum(log_f, axis=1), (0, 2, 1))
    kpos = jnp.arange(S)

    def block(i):
        start = i * BLOCK
        qb = lax.dynamic_slice_in_dim(q, start, BLOCK, axis=1)
        Fq = lax.dynamic_slice_in_dim(F, start, BLOCK, axis=2)
        s = jnp.einsum('bqhd,bkhd->bhqk', qb, k).astype(jnp.float32)
        s = s + Fq[..., None] - F[:, :, None, :]
        qpos = start + jnp.arange(BLOCK)
        s = jnp.where(kpos[None, :] <= qpos[:, None], s, -jnp.inf)
        p = jax.nn.softmax(s, axis=-1)
        return jnp.einsum('bhqk,bkhd->bqhd', p.astype(v.dtype), v)

    out = lax.map(block, jnp.arange(nblk))
    out = jnp.moveaxis(out, 0, 1).reshape(B, S, HD)
    return out @ w_out


def setup_inputs(seed: int = 0) -> dict:
    key = jax.random.key(seed)
    ks = jax.random.split(key, 20)
    D = D_MODEL

    def nrm(k, shape, s):
        return jax.random.normal(k, shape, jnp.float32) * s

    return {
        'x': nrm(ks[0], (BATCH, SEQ, D), 1.0),
        'c': nrm(ks[1], (BATCH, D), 1.0),
        'ada_w': nrm(ks[2], (DEPTH, D, 9 * D), 0.5 * D ** -0.5),
        'ada_b': nrm(ks[3], (DEPTH, 9 * D), 0.02),
        'norm_g': 1.0 + nrm(ks[4], (DEPTH, 3, D), 0.02),
        'ffn_w13': nrm(ks[5], (DEPTH, 2, D, 2 * D_FF), D ** -0.5),
        'ffn_w2': nrm(ks[6], (DEPTH, 2, D_FF, D), D_FF ** -0.5),
        'rel_bias': nrm(ks[7], (REL_BUCKETS, A_Q_HEADS), 0.5),
        'swa_w_in': nrm(ks[8], (N_A_LAYERS, D, A_IN), D ** -0.5),
        'swa_w_out': nrm(ks[9], (N_A_LAYERS, A_Q_HEADS * HEAD_DIM, D), (A_Q_HEADS * HEAD_DIM) ** -0.5),
        'swa_q_g': 1.0 + nrm(ks[10], (N_A_LAYERS, HEAD_DIM), 0.02),
        'swa_k_g': 1.0 + nrm(ks[11], (N_A_LAYERS, HEAD_DIM), 0.02),
        'swa_sink': nrm(ks[12], (N_A_LAYERS, A_Q_HEADS), 0.5),
        'fox_w_in': nrm(ks[13], (N_B_LAYERS, D, B_IN), D ** -0.5),
        'fox_w_out': nrm(ks[14], (N_B_LAYERS, B_HEADS * HEAD_DIM, D), (B_HEADS * HEAD_DIM) ** -0.5),
        'fox_b_f': FORGET_BIAS_INIT + nrm(ks[15], (N_B_LAYERS, B_HEADS), 0.1),
        'fox_q_g': 1.0 + nrm(ks[16], (N_B_LAYERS, HEAD_DIM), 0.02),
        'fox_k_g': 1.0 + nrm(ks[17], (N_B_LAYERS, HEAD_DIM), 0.02),
    }


def reference(x, c, ada_w, ada_b, norm_g, ffn_w13, ffn_w2, rel_bias,
              swa_w_in, swa_w_out, swa_q_g, swa_k_g, swa_sink,
              fox_w_in, fox_w_out, fox_b_f, fox_q_g, fox_k_g):
    B = x.shape[0]
    c_act = jax.nn.silu(c)
    for layer in range(DEPTH):
        mod = (c_act @ ada_w[layer] + ada_b[layer]).reshape(B, 3, 3, D_MODEL)
        h = modulate(x, norm_g[layer, 0], mod[:, 0, 0], mod[:, 0, 1])
        x = x + 0.5 * mod[:, 0, 2][:, None, :] * swiglu(h, ffn_w13[layer, 0], ffn_w2[layer, 0])
        h = modulate(x, norm_g[layer, 1], mod[:, 1, 0], mod[:, 1, 1])
        j = layer // N_MIXERS
        if layer % N_MIXERS == 0:
            y = swa_mixer(h, swa_w_in[j], swa_w_out[j], swa_q_g[j], swa_k_g[j], swa_sink[j], rel_bias)
        else:
            y = fox_mixer(h, fox_w_in[j], fox_w_out[j], fox_b_f[j], fox_q_g[j], fox_k_g[j])
        x = x + mod[:, 1, 2][:, None, :] * y
        h = modulate(x, norm_g[layer, 2], mod[:, 2, 0], mod[:, 2, 1])
        x = x + 0.5 * mod[:, 2, 2][:, None, :] * swiglu(h, ffn_w13[layer, 1], ffn_w2[layer, 1])
    return x
```

```python
import functools
import math

import numpy as np
import jax
import jax.numpy as jnp
from jax import lax
from jax.experimental import pallas as pl
from jax.experimental.pallas import tpu as pltpu

D_MODEL = 1024
DEPTH = 4
HEAD_DIM = 64
N_HEADS = D_MODEL // HEAD_DIM
A_KV_HEADS = 4
A_GROUP = N_HEADS // A_KV_HEADS
WINDOW = 128
BLOCK = 128
REL_BUCKETS = 32
REL_MAX_DIST = 128
D_FF = 2816
EPS = 1e-6

LANES = 128
FF_CHUNK = 256
N_FF_CHUNKS = D_FF // FF_CHUNK
ROW_TILE = 512
ATT_TILE = 512
NEG = -1e30
VMEM_LIMIT = 56 * 1024 * 1024

BF16 = jnp.bfloat16
F32 = jnp.float32


def _params(n_axes):
    return pltpu.CompilerParams(
        dimension_semantics=("arbitrary",) * n_axes, vmem_limit_bytes=VMEM_LIMIT)


def _resident(shape, index_map):
    return pl.BlockSpec(shape, index_map, pipeline_mode=pl.Buffered(1))


def _modulated(x, g, shift, scale):
    ms = jnp.mean(x * x, axis=-1, keepdims=True)
    return (x * lax.rsqrt(ms + EPS) * g) * (1.0 + scale) + shift


def _ada_kernel(c_ref, w_ref, b_ref, o_ref):
    c = c_ref[...]
    c_act = c * jax.nn.sigmoid(c)
    o_ref[...] = jnp.dot(c_act, w_ref[...], precision=lax.Precision.HIGHEST,
                         preferred_element_type=F32) + b_ref[...]


def _ada(c, ada_w, ada_b):
    bsz = c.shape[0]
    n = ada_w.shape[-1]
    tn = D_MODEL
    return pl.pallas_call(
        _ada_kernel,
        grid=(DEPTH, n // tn),
        in_specs=[
            pl.BlockSpec((bsz, D_MODEL), lambda l, j: (0, 0)),
            pl.BlockSpec((None, D_MODEL, tn), lambda l, j: (l, 0, j)),
            pl.BlockSpec((None, 1, tn), lambda l, j: (l, 0, j)),
        ],
        out_specs=pl.BlockSpec((None, bsz, tn), lambda l, j: (l, 0, j)),
        out_shape=jax.ShapeDtypeStruct((DEPTH, bsz, n), F32),
        compiler_params=_params(2),
        name="ada_mod",
    )(c, ada_w, ada_b.reshape(DEPTH, 1, n))


def _ffn_kernel(x_ref, shift_ref, scale_ref, gate_ref, g_ref, wg_ref, wu_ref, w2_ref,
                o_ref, acc_ref):
    x = x_ref[...]
    hb = _modulated(x, g_ref[...], shift_ref[...], scale_ref[...]).astype(BF16)
    acc_ref[...] = jnp.zeros_like(acc_ref)

    def chunk(c, carry):
        gate = jnp.dot(hb, wg_ref[c], preferred_element_type=F32)
        up = jnp.dot(hb, wu_ref[c], preferred_element_type=F32)
        act = (gate * jax.nn.sigmoid(gate) * up).astype(BF16)
        acc_ref[...] += jnp.dot(act, w2_ref[c], preferred_element_type=F32)
        return carry

    lax.fori_loop(0, N_FF_CHUNKS, chunk, 0)
    o_ref[...] = x + (0.5 * gate_ref[...]) * acc_ref[...]


def _ffn(x, shift, scale, gate, g, wg, wu, w2):
    bsz, seq, d = x.shape
    tm = ROW_TILE
    row = pl.BlockSpec((None, tm, d), lambda b, i: (b, i, 0))
    vec = pl.BlockSpec((None, 1, d), lambda b, i: (b, 0, 0))
    return pl.pallas_call(
        _ffn_kernel,
        grid=(bsz, seq // tm),
        in_specs=[
            row, vec, vec, vec,
            pl.BlockSpec((1, d), lambda b, i: (0, 0)),
            _resident((N_FF_CHUNKS, d, FF_CHUNK), lambda b, i: (0, 0, 0)),
            _resident((N_FF_CHUNKS, d, FF_CHUNK), lambda b, i: (0, 0, 0)),
            _resident((N_FF_CHUNKS, FF_CHUNK, d), lambda b, i: (0, 0, 0)),
        ],
        out_specs=row,
        out_shape=jax.ShapeDtypeStruct(x.shape, F32),
        scratch_shapes=[pltpu.VMEM((tm, d), F32)],
        compiler_params=_params(2),
        name="ffn_half_step",
    )(x, shift, scale, gate, g, wg, wu, w2)


def _head_norm(t, gain):
    lo = lax.broadcasted_iota(jnp.int32, t.shape, 1) < HEAD_DIM
    t2 = t * t
    ss_lo = jnp.sum(jnp.where(lo, t2, 0.0), axis=-1, keepdims=True)
    ss_hi = jnp.sum(jnp.where(lo, 0.0, t2), axis=-1, keepdims=True)
    rs = jnp.where(lo, lax.rsqrt(ss_lo * (1.0 / HEAD_DIM) + EPS),
                   lax.rsqrt(ss_hi * (1.0 / HEAD_DIM) + EPS))
    return t * rs * gain


def _swa_proj_kernel(x_ref, shift_ref, scale_ref, g_ref, w_ref, qg_ref, kg_ref,
                     q_ref, k_ref, v_ref):
    hb = _modulated(x_ref[...], g_ref[...], shift_ref[...], scale_ref[...]).astype(BF16)
    lo = lax.broadcasted_iota(jnp.int32, (hb.shape[0], LANES), 1) < HEAD_DIM
    n_q = D_MODEL // LANES
    for c in range(n_q):
        t = jnp.dot(hb, w_ref[:, c * LANES:(c + 1) * LANES], preferred_element_type=F32)
        q_ref[:, c * LANES:(c + 1) * LANES] = _head_norm(t, qg_ref[...]).astype(BF16)
    kv_cols = A_KV_HEADS * HEAD_DIM
    for c in range(kv_cols // LANES):
        t = jnp.dot(hb, w_ref[:, D_MODEL + c * LANES:D_MODEL + (c + 1) * LANES],
                    preferred_element_type=F32)
        t = _head_norm(t, kg_ref[...])
        r = pltpu.roll(t, HEAD_DIM, axis=1)
        k_ref[:, (2 * c) * LANES:(2 * c + 1) * LANES] = jnp.where(lo, t, r).astype(BF16)
        k_ref[:, (2 * c + 1) * LANES:(2 * c + 2) * LANES] = jnp.where(lo, r, t).astype(BF16)
        t = jnp.dot(hb, w_ref[:, D_MODEL + kv_cols + c * LANES:
                              D_MODEL + kv_cols + (c + 1) * LANES],
                    preferred_element_type=F32)
        r = pltpu.roll(t, HEAD_DIM, axis=1)
        v_ref[:, (2 * c) * LANES:(2 * c + 1) * LANES] = jnp.where(lo, t, r).astype(BF16)
        v_ref[:, (2 * c + 1) * LANES:(2 * c + 2) * LANES] = jnp.where(lo, r, t).astype(BF16)


def _swa_proj(x, shift, scale, g, w_in, qg, kg):
    bsz, seq, d = x.shape
    tm = ROW_TILE
    n_in = w_in.shape[1]
    kvw = A_KV_HEADS * LANES
    row = lambda n: pl.BlockSpec((None, tm, n), lambda b, i: (b, i, 0))
    vec = pl.BlockSpec((None, 1, d), lambda b, i: (b, 0, 0))
    const = lambda n: pl.BlockSpec((1, n), lambda b, i: (0, 0))
    return pl.pallas_call(
        _swa_proj_kernel,
        grid=(bsz, seq // tm),
        in_specs=[row(d), vec, vec, const(d),
                  _resident((d, n_in), lambda b, i: (0, 0)),
                  const(LANES), const(LANES)],
        out_specs=[row(d), row(kvw), row(kvw)],
        out_shape=[jax.ShapeDtypeStruct((bsz, seq, d), BF16),
                   jax.ShapeDtypeStruct((bsz, seq, kvw), BF16),
                   jax.ShapeDtypeStruct((bsz, seq, kvw), BF16)],
        compiler_params=_params(2),
        name="swa_proj",
    )(x, shift, scale, g, w_in, qg, kg)


def _swa_attn_kernel(sink_ref, q_ref, k_ref, v_ref, bias_ref, x_ref, gate_ref, wo_ref,
                     o_ref, att_ref):
    tq = q_ref.shape[0]
    q0 = pl.program_id(1) * tq
    lo = lax.broadcasted_iota(jnp.int32, (BLOCK, LANES), 1) < HEAD_DIM
    for sb in range(tq // BLOCK):
        row0 = q0 + sb * BLOCK
        kstart = pl.multiple_of(jnp.maximum(row0 - BLOCK, 0), BLOCK)
        first = (row0 == 0).astype(jnp.int32)
        rows = slice(sb * BLOCK, (sb + 1) * BLOCK)
        for j in range(A_KV_HEADS):
            kc = k_ref[pl.ds(kstart, 2 * BLOCK), j * LANES:(j + 1) * LANES]
            vc = v_ref[pl.ds(kstart, 2 * BLOCK), j * LANES:(j + 1) * LANES]
            qs, sinks = [], []
            for gi in range(A_GROUP):
                h = j * A_GROUP + gi
                qc = q_ref[rows, (h // 2) * LANES:(h // 2 + 1) * LANES]
                keep = lo if h % 2 == 0 else jnp.logical_not(lo)
                qs.append(jnp.where(keep, qc, jnp.zeros_like(qc)))
                sinks.append(jnp.full((BLOCK, 1), sink_ref[h], F32))
            qst = jnp.concatenate(qs, axis=0)
            sink = jnp.concatenate(sinks, axis=0)
            s = lax.dot_general(qst, kc, (((1,), (1,)), ((), ())),
                                preferred_element_type=F32)
            s = s + bias_ref[first, j]
            m = jnp.maximum(jnp.max(s, axis=-1, keepdims=True), sink)
            p = jnp.exp(s - m)
            denom = jnp.sum(p, axis=-1, keepdims=True) + jnp.exp(sink - m)
            o = jnp.dot(p.astype(BF16), vc, preferred_element_type=F32) / denom
            for half in range(2):
                even = o[(2 * half) * BLOCK:(2 * half + 1) * BLOCK]
                odd = o[(2 * half + 1) * BLOCK:(2 * half + 2) * BLOCK]
                ch = 2 * j + half
                att_ref[rows, ch * LANES:(ch + 1) * LANES] = (
                    jnp.where(lo, even, odd).astype(BF16))
    y = jnp.dot(att_ref[...], wo_ref[...], preferred_element_type=F32)
    o_ref[...] = x_ref[...] + gate_ref[...] * y


def _swa_attn(sink, q, k2, v2, bias, x, gate, w_out):
    bsz, seq, d = x.shape
    tq = ROW_TILE
    kvw = k2.shape[-1]
    row = lambda n: pl.BlockSpec((None, tq, n), lambda b, i: (b, i, 0))
    return pl.pallas_call(
        _swa_attn_kernel,
        grid=(bsz, seq // tq),
        in_specs=[
            pl.BlockSpec(memory_space=pltpu.SMEM),
            row(d),
            _resident((None, seq, kvw), lambda b, i: (b, 0, 0)),
            _resident((None, seq, kvw), lambda b, i: (b, 0, 0)),
            _resident(bias.shape, lambda b, i: (0, 0, 0, 0)),
            row(d),
            pl.BlockSpec((None, 1, d), lambda b, i: (b, 0, 0)),
            _resident((d, d), lambda b, i: (0, 0)),
        ],
        out_specs=row(d),
        out_shape=jax.ShapeDtypeStruct(x.shape, F32),
        scratch_shapes=[pltpu.VMEM((tq, d), BF16)],
        compiler_params=_params(2),
        name="swa_attn_out",
    )(sink, q, k2, v2, bias, x, gate, w_out)


def _rel_bucket_table():
    n = np.arange(WINDOW)
    max_exact = REL_BUCKETS // 2
    nf = np.maximum(n, 1).astype(np.float32)
    large = max_exact + (np.log(nf / max_exact) / math.log(REL_MAX_DIST / max_exact)
                         * (REL_BUCKETS - max_exact)).astype(np.int32)
    large = np.minimum(large, REL_BUCKETS - 1)
    return np.where(n < max_exact, n, large)


def _swa_bias_tiles(rel_bias):
    bucket = _rel_bucket_table()
    qi = np.arange(BLOCK)[:, None]
    kj = np.arange(2 * BLOCK)[None, :]
    tiles = []
    for key_offset in (BLOCK, 0):
        dist = qi + key_offset - kj
        valid = (dist >= 0) & (dist < WINDOW)
        idx = bucket[np.clip(dist, 0, WINDOW - 1)]
        b = jnp.transpose(rel_bias.astype(F32)[idx], (2, 0, 1))
        b = jnp.where(valid[None], b, NEG)
        tiles.append(b.reshape(A_KV_HEADS, A_GROUP * BLOCK, 2 * BLOCK))
    return jnp.stack(tiles)


def _fox_proj_kernel(x_ref, shift_ref, scale_ref, g_ref, w_ref, bf_ref, qg_ref, kg_ref,
                     q_ref, k_ref, v_ref, ft_ref, carry_ref):
    tm = x_ref.shape[0]
    hb = _modulated(x_ref[...], g_ref[...], shift_ref[...], scale_ref[...]).astype(BF16)
    n_c = D_MODEL // LANES
    for c in range(n_c):
        t = jnp.dot(hb, w_ref[:, c * LANES:(c + 1) * LANES], preferred_element_type=F32)
        q_ref[:, c * LANES:(c + 1) * LANES] = _head_norm(t, qg_ref[...]).astype(BF16)
        t = jnp.dot(hb, w_ref[:, D_MODEL + c * LANES:D_MODEL + (c + 1) * LANES],
                    preferred_element_type=F32)
        k_ref[:, c * LANES:(c + 1) * LANES] = _head_norm(t, kg_ref[...]).astype(BF16)
        v_ref[:, c * LANES:(c + 1) * LANES] = jnp.dot(
            hb, w_ref[:, 2 * D_MODEL + c * LANES:2 * D_MODEL + (c + 1) * LANES],
            preferred_element_type=F32).astype(BF16)

    z = jnp.dot(hb, w_ref[:, 3 * D_MODEL:3 * D_MODEL + LANES],
                preferred_element_type=F32) + bf_ref[...]
    log_f = jnp.minimum(z, 0.0) - jnp.log1p(jnp.exp(-jnp.abs(z)))
    r = lax.broadcasted_iota(jnp.int32, (tm, tm), 0)
    cidx = lax.broadcasted_iota(jnp.int32, (tm, tm), 1)
    tri = (cidx <= r).astype(F32)

    @pl.when(pl.program_id(1) == 0)
    def _():
        carry_ref[...] = jnp.zeros_like(carry_ref)

    cum = jnp.dot(tri, log_f, precision=lax.Precision.HIGHEST,
                  preferred_element_type=F32) + carry_ref[...]
    carry_ref[...] = cum[tm - 1:tm, :]
    ft_ref[...] = cum.T[:N_HEADS, :]


def _fox_proj(x, shift, scale, g, w_in, b_f, qg, kg):
    bsz, seq, d = x.shape
    tm = ROW_TILE
    n_in = w_in.shape[1]
    row = pl.BlockSpec((None, tm, d), lambda b, i: (b, i, 0))
    vec = pl.BlockSpec((None, 1, d), lambda b, i: (b, 0, 0))
    const = lambda n: pl.BlockSpec((1, n), lambda b, i: (0, 0))
    act = jax.ShapeDtypeStruct((bsz, seq, d), BF16)
    return pl.pallas_call(
        _fox_proj_kernel,
        grid=(bsz, seq // tm),
        in_specs=[row, vec, vec, const(d),
                  _resident((d, n_in), lambda b, i: (0, 0)),
                  const(LANES), const(LANES), const(LANES)],
        out_specs=[row, row, row,
                   pl.BlockSpec((None, N_HEADS, tm), lambda b, i: (b, 0, i))],
        out_shape=[act, act, act, jax.ShapeDtypeStruct((bsz, N_HEADS, seq), F32)],
        scratch_shapes=[pltpu.VMEM((1, LANES), F32)],
        compiler_params=_params(2),
        name="fox_proj",
    )(x, shift, scale, g, w_in, b_f, qg, kg)


def _fox_attn_kernel(q_ref, k_ref, v_ref, f_ref, o_ref, m_ref, l_ref, acc_ref):
    t = q_ref.shape[0]
    qi = pl.program_id(2)
    q0 = pl.multiple_of(qi * t, t)
    q = q_ref[...]
    lo = lax.broadcasted_iota(jnp.int32, (t, LANES), 1) < HEAD_DIM
    zero = jnp.zeros_like(q)
    qh = (jnp.where(lo, q, zero), jnp.where(lo, zero, q))
    m_ref[...] = jnp.full_like(m_ref, NEG)
    l_ref[...] = jnp.zeros_like(l_ref)
    acc_ref[...] = jnp.zeros_like(acc_ref)
    causal = (lax.broadcasted_iota(jnp.int32, (t, t), 1)
              <= lax.broadcasted_iota(jnp.int32, (t, t), 0))

    def step(kb, masked):
        k0 = pl.multiple_of(kb * t, t)
        kc = k_ref[pl.ds(k0, t), :]
        vc = v_ref[pl.ds(k0, t), :]
        for a in range(2):
            f_rel = f_ref[a:a + 1, pl.ds(k0, t)] - f_ref[a:a + 1, pl.ds(q0, LANES)][:, :1]
            s = lax.dot_general(qh[a], kc, (((1,), (1,)), ((), ())),
                                preferred_element_type=F32) - f_rel
            if masked:
                s = jnp.where(causal, s, NEG)
            m_old = m_ref[a]
            m_new = jnp.maximum(m_old, jnp.max(s, axis=-1, keepdims=True))
            alpha = jnp.exp(m_old - m_new)
            p = jnp.exp(s - m_new)
            l_ref[a] = alpha * l_ref[a] + jnp.sum(p, axis=-1, keepdims=True)
            acc_ref[a] = alpha * acc_ref[a] + jnp.dot(p.astype(BF16), vc,
                                                      preferred_element_type=F32)
            m_ref[a] = m_new

    def body(kb, carry):
        step(kb, False)
        return carry

    lax.fori_loop(0, qi, body, 0)
    step(qi, True)
    o_ref[...] = jnp.where(lo, acc_ref[0] / l_ref[0], acc_ref[1] / l_ref[1]).astype(BF16)


def _fox_attn(q, k, v, ft):
    bsz, seq, d = q.shape
    t = ATT_TILE
    n_pairs = d // LANES
    ft = ft.reshape(bsz, n_pairs, 2, seq)
    return pl.pallas_call(
        _fox_attn_kernel,
        grid=(bsz, n_pairs, seq // t),
        in_specs=[
            pl.BlockSpec((None, t, LANES), lambda b, h, i: (b, i, h)),
            pl.BlockSpec((None, seq, LANES), lambda b, h, i: (b, 0, h)),
            pl.BlockSpec((None, seq, LANES), lambda b, h, i: (b, 0, h)),
            pl.BlockSpec((None, None, 2, seq), lambda b, h, i: (b, h, 0, 0)),
        ],
        out_specs=pl.BlockSpec((None, t, LANES), lambda b, h, i: (b, i, h)),
        out_shape=jax.ShapeDtypeStruct((bsz, seq, d), BF16),
        scratch_shapes=[pltpu.VMEM((2, t, 1), F32), pltpu.VMEM((2, t, 1), F32),
                        pltpu.VMEM((2, t, LANES), F32)],
        compiler_params=_params(3),
        name="fox_attn",
    )(q, k, v, ft)


def _out_proj_kernel(a_ref, x_ref, gate_ref, wo_ref, o_ref):
    y = jnp.dot(a_ref[...], wo_ref[...], preferred_element_type=F32)
    o_ref[...] = x_ref[...] + gate_ref[...] * y


def _out_proj(att, x, gate, w_out):
    bsz, seq, d = x.shape
    tm = ROW_TILE
    row = pl.BlockSpec((None, tm, d), lambda b, i: (b, i, 0))
    return pl.pallas_call(
        _out_proj_kernel,
        grid=(bsz, seq // tm),
        in_specs=[row, row, pl.BlockSpec((None, 1, d), lambda b, i: (b, 0, 0)),
                  _resident((d, d), lambda b, i: (0, 0))],
        out_specs=row,
        out_shape=jax.ShapeDtypeStruct(x.shape, F32),
        compiler_params=_params(2),
        name="out_proj",
    )(att, x, gate, w_out)


def _head_gain(gain, scale):
    return (jnp.tile(gain.astype(F32), LANES // HEAD_DIM) * scale).reshape(1, LANES)


def kernel(x, c, ada_w, ada_b, norm_g, ffn_w13, ffn_w2, rel_bias, swa_w_in, swa_w_out,
           swa_q_g, swa_k_g, swa_sink, fox_w_in, fox_w_out, fox_b_f, fox_q_g, fox_k_g):
    bsz = x.shape[0]
    mod = _ada(c, ada_w, ada_b).reshape(DEPTH, bsz, 3, 3, 1, D_MODEL)
    q_scale = HEAD_DIM ** -0.5
    bias_tiles = _swa_bias_tiles(rel_bias)

    def chunked_ffn(layer, half):
        w13 = ffn_w13[layer, half].astype(BF16)
        wg = w13[:, :D_FF].reshape(D_MODEL, N_FF_CHUNKS, FF_CHUNK).transpose(1, 0, 2)
        wu = w13[:, D_FF:].reshape(D_MODEL, N_FF_CHUNKS, FF_CHUNK).transpose(1, 0, 2)
        w2 = ffn_w2[layer, half].astype(BF16).reshape(N_FF_CHUNKS, FF_CHUNK, D_MODEL)
        return wg, wu, w2

    for layer in range(DEPTH):
        g = norm_g[layer].reshape(3, 1, D_MODEL)
        m = mod[layer]
        x = _ffn(x, m[:, 0, 0], m[:, 0, 1], m[:, 0, 2], g[0], *chunked_ffn(layer, 0))
        j = layer // 2
        if layer % 2 == 0:
            q, k2, v2 = _swa_proj(x, m[:, 1, 0], m[:, 1, 1], g[1], swa_w_in[j].astype(BF16),
                                  _head_gain(swa_q_g[j], q_scale), _head_gain(swa_k_g[j], 1.0))
            x = _swa_attn(swa_sink[j].astype(F32), q, k2, v2, bias_tiles, x, m[:, 1, 2],
                          swa_w_out[j].astype(BF16))
        else:
            w_in = jnp.pad(fox_w_in[j], ((0, 0), (0, LANES - N_HEADS))).astype(BF16)
            b_f = jnp.pad(fox_b_f[j].astype(F32), (0, LANES - N_HEADS)).reshape(1, LANES)
            q, k, v, ft = _fox_proj(x, m[:, 1, 0], m[:, 1, 1], g[1], w_in, b_f,
                                    _head_gain(fox_q_g[j], q_scale),
                                    _head_gain(fox_k_g[j], 1.0))
            att = _fox_attn(q, k, v, ft)
            x = _out_proj(att, x, m[:, 1, 2], fox_w_out[j].astype(BF16))
        x = _ffn(x, m[:, 2, 0], m[:, 2, 1], m[:, 2, 2], g[2], *chunked_ffn(layer, 1))
    return x
```

```python
import functools
import math

import numpy as np
import jax
import jax.numpy as jnp
from jax import lax
from jax.experimental import pallas as pl
from jax.experimental.pallas import tpu as pltpu

D_MODEL = 1024
DEPTH = 4
HEAD_DIM = 64
N_HEADS = D_MODEL // HEAD_DIM
A_KV_HEADS = 4
A_GROUP = N_HEADS // A_KV_HEADS
WINDOW = 128
BLOCK = 128
REL_BUCKETS = 32
REL_MAX_DIST = 128
D_FF = 2816
EPS = 1e-6

LANES = 128
FF_CHUNK = 256
N_FF_CHUNKS = D_FF // FF_CHUNK
ROW_TILE = 512
ATT_Q_TILE = 1024
ATT_K_TILE = 512
ATT_COL_TILE = 256
ONES_ROWS = 16
LOG2E = 1.4426950408889634
NEG = -1e30
VMEM_LIMIT = 56 * 1024 * 1024

BF16 = jnp.bfloat16
F32 = jnp.float32


def _params(n_axes):
    return pltpu.CompilerParams(
        dimension_semantics=("arbitrary",) * n_axes, vmem_limit_bytes=VMEM_LIMIT)


def _resident(shape, index_map):
    return pl.BlockSpec(shape, index_map, pipeline_mode=pl.Buffered(1))


def _modulated(x, g, shift, scale):
    ms = jnp.mean(x * x, axis=-1, keepdims=True)
    return (x * lax.rsqrt(ms + EPS) * g) * (1.0 + scale) + shift


def _ada_kernel(c_ref, w_ref, b_ref, o_ref):
    c = c_ref[...]
    c_act = c * jax.nn.sigmoid(c)
    o_ref[...] = jnp.dot(c_act, w_ref[...], precision=lax.Precision.HIGHEST,
                         preferred_element_type=F32) + b_ref[...]


def _ada(c, ada_w, ada_b):
    bsz = c.shape[0]
    n = ada_w.shape[-1]
    tn = D_MODEL
    return pl.pallas_call(
        _ada_kernel,
        grid=(DEPTH, n // tn),
        in_specs=[
            pl.BlockSpec((bsz, D_MODEL), lambda l, j: (0, 0)),
            pl.BlockSpec((None, D_MODEL, tn), lambda l, j: (l, 0, j)),
            pl.BlockSpec((None, 1, tn), lambda l, j: (l, 0, j)),
        ],
        out_specs=pl.BlockSpec((None, bsz, tn), lambda l, j: (l, 0, j)),
        out_shape=jax.ShapeDtypeStruct((DEPTH, bsz, n), F32),
        compiler_params=_params(2),
        name="ada_mod",
    )(c, ada_w, ada_b.reshape(DEPTH, 1, n))


def _ffn_kernel(x_ref, shift_ref, scale_ref, gate_ref, g_ref, wg_ref, wu_ref, w2_ref,
                o_ref, acc_ref):
    x = x_ref[...]
    hb = _modulated(x, g_ref[...], shift_ref[...], scale_ref[...]).astype(BF16)
    acc_ref[...] = jnp.zeros_like(acc_ref)

    def chunk(c, carry):
        gate = jnp.dot(hb, wg_ref[c], preferred_element_type=F32)
        up = jnp.dot(hb, wu_ref[c], preferred_element_type=F32)
        act = (gate * jax.nn.sigmoid(gate) * up).astype(BF16)
        acc_ref[...] += jnp.dot(act, w2_ref[c], preferred_element_type=F32)
        return carry

    lax.fori_loop(0, N_FF_CHUNKS, chunk, 0)
    o_ref[...] = x + (0.5 * gate_ref[...]) * acc_ref[...]


def _ffn(x, shift, scale, gate, g, wg, wu, w2):
    bsz, seq, d = x.shape
    tm = ROW_TILE
    row = pl.BlockSpec((None, tm, d), lambda b, i: (b, i, 0))
    vec = pl.BlockSpec((None, 1, d), lambda b, i: (b, 0, 0))
    return pl.pallas_call(
        _ffn_kernel,
        grid=(bsz, seq // tm),
        in_specs=[
            row, vec, vec, vec,
            pl.BlockSpec((1, d), lambda b, i: (0, 0)),
            _resident((N_FF_CHUNKS, d, FF_CHUNK), lambda b, i: (0, 0, 0)),
            _resident((N_FF_CHUNKS, d, FF_CHUNK), lambda b, i: (0, 0, 0)),
            _resident((N_FF_CHUNKS, FF_CHUNK, d), lambda b, i: (0, 0, 0)),
        ],
        out_specs=row,
        out_shape=jax.ShapeDtypeStruct(x.shape, F32),
        scratch_shapes=[pltpu.VMEM((tm, d), F32)],
        compiler_params=_params(2),
        name="ffn_half_step",
    )(x, shift, scale, gate, g, wg, wu, w2)


def _head_norm(t, gain):
    lo = lax.broadcasted_iota(jnp.int32, t.shape, 1) < HEAD_DIM
    t2 = t * t
    ss_lo = jnp.sum(jnp.where(lo, t2, 0.0), axis=-1, keepdims=True)
    ss_hi = jnp.sum(jnp.where(lo, 0.0, t2), axis=-1, keepdims=True)
    rs = jnp.where(lo, lax.rsqrt(ss_lo * (1.0 / HEAD_DIM) + EPS),
                   lax.rsqrt(ss_hi * (1.0 / HEAD_DIM) + EPS))
    return t * rs * gain


def _swa_proj_kernel(x_ref, shift_ref, scale_ref, g_ref, w_ref, qg_ref, kg_ref,
                     q_ref, k_ref, v_ref):
    hb = _modulated(x_ref[...], g_ref[...], shift_ref[...], scale_ref[...]).astype(BF16)
    lo = lax.broadcasted_iota(jnp.int32, (hb.shape[0], LANES), 1) < HEAD_DIM
    n_q = D_MODEL // LANES
    for c in range(n_q):
        t = jnp.dot(hb, w_ref[:, c * LANES:(c + 1) * LANES], preferred_element_type=F32)
        q_ref[:, c * LANES:(c + 1) * LANES] = _head_norm(t, qg_ref[...]).astype(BF16)
    kv_cols = A_KV_HEADS * HEAD_DIM
    for c in range(kv_cols // LANES):
        t = jnp.dot(hb, w_ref[:, D_MODEL + c * LANES:D_MODEL + (c + 1) * LANES],
                    preferred_element_type=F32)
        t = _head_norm(t, kg_ref[...])
        r = pltpu.roll(t, HEAD_DIM, axis=1)
        k_ref[:, (2 * c) * LANES:(2 * c + 1) * LANES] = jnp.where(lo, t, r).astype(BF16)
        k_ref[:, (2 * c + 1) * LANES:(2 * c + 2) * LANES] = jnp.where(lo, r, t).astype(BF16)
        t = jnp.dot(hb, w_ref[:, D_MODEL + kv_cols + c * LANES:
                              D_MODEL + kv_cols + (c + 1) * LANES],
                    preferred_element_type=F32)
        r = pltpu.roll(t, HEAD_DIM, axis=1)
        v_ref[:, (2 * c) * LANES:(2 * c + 1) * LANES] = jnp.where(lo, t, r).astype(BF16)
        v_ref[:, (2 * c + 1) * LANES:(2 * c + 2) * LANES] = jnp.where(lo, r, t).astype(BF16)


def _swa_proj(x, shift, scale, g, w_in, qg, kg):
    bsz, seq, d = x.shape
    tm = ROW_TILE
    n_in = w_in.shape[1]
    kvw = A_KV_HEADS * LANES
    row = lambda n: pl.BlockSpec((None, tm, n), lambda b, i: (b, i, 0))
    vec = pl.BlockSpec((None, 1, d), lambda b, i: (b, 0, 0))
    const = lambda n: pl.BlockSpec((1, n), lambda b, i: (0, 0))
    return pl.pallas_call(
        _swa_proj_kernel,
        grid=(bsz, seq // tm),
        in_specs=[row(d), vec, vec, const(d),
                  _resident((d, n_in), lambda b, i: (0, 0)),
                  const(LANES), const(LANES)],
        out_specs=[row(d), row(kvw), row(kvw)],
        out_shape=[jax.ShapeDtypeStruct((bsz, seq, d), BF16),
                   jax.ShapeDtypeStruct((bsz, seq, kvw), BF16),
                   jax.ShapeDtypeStruct((bsz, seq, kvw), BF16)],
        compiler_params=_params(2),
        name="swa_proj",
    )(x, shift, scale, g, w_in, qg, kg)


def _swa_attn_kernel(sink_ref, q_ref, k_ref, v_ref, bias_ref, x_ref, gate_ref, wo_ref,
                     o_ref, att_ref):
    tq = q_ref.shape[0]
    q0 = pl.program_id(1) * tq
    lo = lax.broadcasted_iota(jnp.int32, (BLOCK, LANES), 1) < HEAD_DIM
    for sb in range(tq // BLOCK):
        row0 = q0 + sb * BLOCK
        kstart = pl.multiple_of(jnp.maximum(row0 - BLOCK, 0), BLOCK)
        first = (row0 == 0).astype(jnp.int32)
        rows = slice(sb * BLOCK, (sb + 1) * BLOCK)
        for j in range(A_KV_HEADS):
            kc = k_ref[pl.ds(kstart, 2 * BLOCK), j * LANES:(j + 1) * LANES]
            vc = v_ref[pl.ds(kstart, 2 * BLOCK), j * LANES:(j + 1) * LANES]
            qs, sinks = [], []
            for gi in range(A_GROUP):
                h = j * A_GROUP + gi
                qc = q_ref[rows, (h // 2) * LANES:(h // 2 + 1) * LANES]
                keep = lo if h % 2 == 0 else jnp.logical_not(lo)
                qs.append(jnp.where(keep, qc, jnp.zeros_like(qc)))
                sinks.append(jnp.full((BLOCK, 1), sink_ref[h], F32))
            qst = jnp.concatenate(qs, axis=0)
            sink = jnp.concatenate(sinks, axis=0)
            s = lax.dot_general(qst, kc, (((1,), (1,)), ((), ())),
                                preferred_element_type=F32)
            s = s + bias_ref[first, j]
            m = jnp.maximum(jnp.max(s, axis=-1, keepdims=True), sink)
            p = jnp.exp(s - m)
            denom = jnp.sum(p, axis=-1, keepdims=True) + jnp.exp(sink - m)
            o = jnp.dot(p.astype(BF16), vc, preferred_element_type=F32) / denom
            for half in range(2):
                even = o[(2 * half) * BLOCK:(2 * half + 1) * BLOCK]
                odd = o[(2 * half + 1) * BLOCK:(2 * half + 2) * BLOCK]
                ch = 2 * j + half
                att_ref[rows, ch * LANES:(ch + 1) * LANES] = (
                    jnp.where(lo, even, odd).astype(BF16))
    y = jnp.dot(att_ref[...], wo_ref[...], preferred_element_type=F32)
    o_ref[...] = x_ref[...] + gate_ref[...] * y


def _swa_attn(sink, q, k2, v2, bias, x, gate, w_out):
    bsz, seq, d = x.shape
    tq = ROW_TILE
    kvw = k2.shape[-1]
    row = lambda n: pl.BlockSpec((None, tq, n), lambda b, i: (b, i, 0))
    return pl.pallas_call(
        _swa_attn_kernel,
        grid=(bsz, seq // tq),
        in_specs=[
            pl.BlockSpec(memory_space=pltpu.SMEM),
            row(d),
            _resident((None, seq, kvw), lambda b, i: (b, 0, 0)),
            _resident((None, seq, kvw), lambda b, i: (b, 0, 0)),
            _resident(bias.shape, lambda b, i: (0, 0, 0, 0)),
            row(d),
            pl.BlockSpec((None, 1, d), lambda b, i: (b, 0, 0)),
            _resident((d, d), lambda b, i: (0, 0)),
        ],
        out_specs=row(d),
        out_shape=jax.ShapeDtypeStruct(x.shape, F32),
        scratch_shapes=[pltpu.VMEM((tq, d), BF16)],
        compiler_params=_params(2),
        name="swa_attn_out",
    )(sink, q, k2, v2, bias, x, gate, w_out)


def _rel_bucket_table():
    n = np.arange(WINDOW)
    max_exact = REL_BUCKETS // 2
    nf = np.maximum(n, 1).astype(np.float32)
    large = max_exact + (np.log(nf / max_exact) / math.log(REL_MAX_DIST / max_exact)
                         * (REL_BUCKETS - max_exact)).astype(np.int32)
    large = np.minimum(large, REL_BUCKETS - 1)
    return np.where(n < max_exact, n, large)


def _swa_bias_tiles(rel_bias):
    bucket = _rel_bucket_table()
    qi = np.arange(BLOCK)[:, None]
    kj = np.arange(2 * BLOCK)[None, :]
    tiles = []
    for key_offset in (BLOCK, 0):
        dist = qi + key_offset - kj
        valid = (dist >= 0) & (dist < WINDOW)
        idx = bucket[np.clip(dist, 0, WINDOW - 1)]
        b = jnp.transpose(rel_bias.astype(F32)[idx], (2, 0, 1))
        b = jnp.where(valid[None], b, NEG)
        tiles.append(b.reshape(A_KV_HEADS, A_GROUP * BLOCK, 2 * BLOCK))
    return jnp.stack(tiles)


def _fox_proj_kernel(x_ref, shift_ref, scale_ref, g_ref, w_ref, bf_ref, qg_ref, kg_ref,
                     q_ref, k_ref, v_ref, ft_ref, carry_ref):
    tm = x_ref.shape[0]
    hb = _modulated(x_ref[...], g_ref[...], shift_ref[...], scale_ref[...]).astype(BF16)
    n_c = D_MODEL // LANES
    for c in range(n_c):
        t = jnp.dot(hb, w_ref[:, c * LANES:(c + 1) * LANES], preferred_element_type=F32)
        q_ref[:, c * LANES:(c + 1) * LANES] = _head_norm(t, qg_ref[...]).astype(BF16)
        t = jnp.dot(hb, w_ref[:, D_MODEL + c * LANES:D_MODEL + (c + 1) * LANES],
                    preferred_element_type=F32)
        k_ref[:, c * LANES:(c + 1) * LANES] = _head_norm(t, kg_ref[...]).astype(BF16)
        v_ref[:, c * LANES:(c + 1) * LANES] = jnp.dot(
            hb, w_ref[:, 2 * D_MODEL + c * LANES:2 * D_MODEL + (c + 1) * LANES],
            preferred_element_type=F32).astype(BF16)

    z = jnp.dot(hb, w_ref[:, 3 * D_MODEL:3 * D_MODEL + LANES],
                preferred_element_type=F32) + bf_ref[...]
    log_f = jnp.minimum(z, 0.0) - jnp.log1p(jnp.exp(-jnp.abs(z)))
    r = lax.broadcasted_iota(jnp.int32, (tm, tm), 0)
    cidx = lax.broadcasted_iota(jnp.int32, (tm, tm), 1)
    tri = (cidx <= r).astype(F32)

    @pl.when(pl.program_id(1) == 0)
    def _():
        carry_ref[...] = jnp.zeros_like(carry_ref)

    cum = jnp.dot(tri, log_f, precision=lax.Precision.HIGHEST,
                  preferred_element_type=F32) + carry_ref[...]
    carry_ref[...] = cum[tm - 1:tm, :]
    ft_ref[...] = cum.T[:N_HEADS, :]


def _fox_proj(x, shift, scale, g, w_in, b_f, qg, kg):
    bsz, seq, d = x.shape
    tm = ROW_TILE
    n_in = w_in.shape[1]
    row = pl.BlockSpec((None, tm, d), lambda b, i: (b, i, 0))
    vec = pl.BlockSpec((None, 1, d), lambda b, i: (b, 0, 0))
    const = lambda n: pl.BlockSpec((1, n), lambda b, i: (0, 0))
    act = jax.ShapeDtypeStruct((bsz, seq, d), BF16)
    return pl.pallas_call(
        _fox_proj_kernel,
        grid=(bsz, seq // tm),
        in_specs=[row, vec, vec, const(d),
                  _resident((d, n_in), lambda b, i: (0, 0)),
                  const(LANES), const(LANES), const(LANES)],
        out_specs=[row, row, row,
                   pl.BlockSpec((None, N_HEADS, tm), lambda b, i: (b, 0, i))],
        out_shape=[act, act, act, jax.ShapeDtypeStruct((bsz, N_HEADS, seq), F32)],
        scratch_shapes=[pltpu.VMEM((1, LANES), F32)],
        compiler_params=_params(2),
        name="fox_proj",
    )(x, shift, scale, g, w_in, b_f, qg, kg)


def _fox_attn_kernel(q_ref, k_ref, v_ref, f_ref, o_ref, vt_ref, wq_ref,
                     s00_ref, s01_ref, s10_ref, s11_ref, p00_ref, p01_ref, p10_ref, p11_ref,
                     m_ref, alpha_ref, acc_ref):
    s_refs = ((s00_ref, s01_ref), (s10_ref, s11_ref))
    p_refs = ((p00_ref, p01_ref), (p10_ref, p11_ref))
    tq = q_ref.shape[0]
    tk = ATT_K_TILE
    seq = k_ref.shape[0]
    qi = pl.program_id(2)

    @pl.when(qi == 0)
    def _():
        for c in range(seq // tk):
            blk = v_ref[c * tk:(c + 1) * tk, :].astype(F32)
            vt_ref[:, c * tk:(c + 1) * tk] = blk.T.astype(BF16)

    q0 = pl.multiple_of(qi * tq, tq)
    qt = q_ref[...].astype(F32).T.astype(BF16)
    zeros = jnp.zeros((HEAD_DIM, tq), BF16)
    wq_ref[0] = jnp.concatenate([qt[:HEAD_DIM], zeros], axis=0)
    wq_ref[1] = jnp.concatenate([zeros, qt[HEAD_DIM:]], axis=0)
    ones = jnp.ones((ONES_ROWS, tk), BF16)
    m_ref[...] = jnp.full_like(m_ref, NEG)
    alpha_ref[...] = jnp.zeros_like(alpha_ref)
    acc_ref[...] = jnp.zeros_like(acc_ref)
    for a in range(2):
        p_refs[1][a][...] = jnp.zeros_like(p_refs[1][a])

    n_ct = tq // ATT_COL_TILE

    def cols(ct):
        return slice(ct * ATT_COL_TILE, (ct + 1) * ATT_COL_TILE)

    def key_bias(j):
        k0 = pl.multiple_of(j * tk, tk)
        out = []
        for a in range(2):
            f_rel = (f_ref[a:a + 1, pl.ds(k0, tk)]
                     - f_ref[a:a + 1, pl.ds(q0, LANES)][:, :1]) * LOG2E
            f_col = jnp.broadcast_to(f_rel, (LANES, tk)).T
            out.append(jnp.concatenate([f_col] * (ATT_COL_TILE // LANES), axis=1))
        return out

    def score_stage(j, slot, ct, bias):
        kc = k_ref[pl.ds(pl.multiple_of(j * tk, tk), tk), :]
        for a in range(2):
            s_refs[slot][a][:, cols(ct)] = (
                jnp.dot(kc, wq_ref[a, :, cols(ct)], preferred_element_type=F32) - bias[a])

    def softmax_stage(j, slot, ct, masked):
        k0 = pl.multiple_of(j * tk, tk)
        for a in range(2):
            s = s_refs[slot][a][:, cols(ct)]
            if masked:
                key = k0 + lax.broadcasted_iota(jnp.int32, s.shape, 0)
                qry = (q0 + ct * ATT_COL_TILE) + lax.broadcasted_iota(jnp.int32, s.shape, 1)
                s = jnp.where(key <= qry, s, NEG)
                s_refs[slot][a][:, cols(ct)] = s
            m_old = m_ref[a, :, cols(ct)]
            m_new = jnp.maximum(m_old, jnp.max(s, axis=0, keepdims=True))
            alpha_ref[a, :, cols(ct)] = jnp.exp2(m_old - m_new)
            p_refs[slot][a][:, cols(ct)] = jnp.exp2(
                s_refs[slot][a][:, cols(ct)] - m_new).astype(BF16)
            m_ref[a, :, cols(ct)] = m_new

    def pv_stage(j, slot, ct):
        k0 = pl.multiple_of(jnp.maximum(j, 0) * tk, tk)
        for a in range(2):
            v_aug = jnp.concatenate(
                [vt_ref[a * HEAD_DIM:(a + 1) * HEAD_DIM, pl.ds(k0, tk)], ones], axis=0)
            acc_ref[a, :, cols(ct)] = (
                alpha_ref[a, :, cols(ct)] * acc_ref[a, :, cols(ct)]
                + jnp.dot(v_aug, p_refs[slot][a][:, cols(ct)], preferred_element_type=F32))

    n_diag = tq // tk
    n_full = qi * n_diag

    def visibility(d, ct):
        if d < 0 or d >= n_diag:
            return "full" if d < 0 else "skip"
        first_key, last_key = d * tk, (d + 1) * tk - 1
        first_qry, last_qry = ct * ATT_COL_TILE, (ct + 1) * ATT_COL_TILE - 1
        if last_key <= first_qry:
            return "full"
        return "skip" if first_key > last_qry else "mask"

    def block_step(j, slot, d=-1):
        nxt = -1 if d < 0 else d + 1
        bias = key_bias(j + 1) if nxt < n_diag else None
        for ct in range(n_ct):
            if visibility(d - 1, ct) != "skip":
                pv_stage(j - 1, 1 - slot, ct)
            if visibility(nxt, ct) != "skip":
                score_stage(j + 1, 1 - slot, ct, bias)
            if visibility(d, ct) != "skip":
                softmax_stage(j, slot, ct, visibility(d, ct) == "mask")

    bias0 = key_bias(0)
    for ct in range(n_ct):
        score_stage(0, 0, ct, bias0)

    def pair(ii, carry):
        block_step(2 * ii, 0)
        block_step(2 * ii + 1, 1)
        return carry

    lax.fori_loop(0, n_full // 2, pair, 0)
    for d in range(n_diag):
        block_step(n_full + d, d % 2, d)
    for ct in range(n_ct):
        if visibility(n_diag - 1, ct) != "skip":
            pv_stage(n_full + n_diag - 1, (n_diag - 1) % 2, ct)

    outs = []
    for a in range(2):
        acc = acc_ref[a]
        outs.append((acc[:HEAD_DIM] / acc[HEAD_DIM:HEAD_DIM + 1]).T)
    o_ref[...] = jnp.concatenate(outs, axis=1).astype(BF16)


def _fox_attn(q, k, v, ft):
    bsz, seq, d = q.shape
    tq, tk = ATT_Q_TILE, ATT_K_TILE
    n_pairs = d // LANES
    ft = ft.reshape(bsz, n_pairs, 2, seq)
    return pl.pallas_call(
        _fox_attn_kernel,
        grid=(bsz, n_pairs, seq // tq),
        in_specs=[
            pl.BlockSpec((None, tq, LANES), lambda b, h, i: (b, i, h)),
            pl.BlockSpec((None, seq, LANES), lambda b, h, i: (b, 0, h)),
            pl.BlockSpec((None, seq, LANES), lambda b, h, i: (b, 0, h)),
            pl.BlockSpec((None, None, 2, seq), lambda b, h, i: (b, h, 0, 0)),
        ],
        out_specs=pl.BlockSpec((None, tq, LANES), lambda b, h, i: (b, i, h)),
        out_shape=jax.ShapeDtypeStruct((bsz, seq, d), BF16),
        scratch_shapes=(
            [pltpu.VMEM((LANES, seq), BF16), pltpu.VMEM((2, LANES, tq), BF16)]
            + [pltpu.VMEM((tk, tq), F32)] * 4 + [pltpu.VMEM((tk, tq), BF16)] * 4
            + [pltpu.VMEM((2, 1, tq), F32), pltpu.VMEM((2, 1, tq), F32),
               pltpu.VMEM((2, HEAD_DIM + ONES_ROWS, tq), F32)]),
        compiler_params=_params(3),
        name="fox_attn",
    )(q, k, v, ft)


def _out_proj_kernel(a_ref, x_ref, gate_ref, wo_ref, o_ref):
    y = jnp.dot(a_ref[...], wo_ref[...], preferred_element_type=F32)
    o_ref[...] = x_ref[...] + gate_ref[...] * y


def _out_proj(att, x, gate, w_out):
    bsz, seq, d = x.shape
    tm = ROW_TILE
    row = pl.BlockSpec((None, tm, d), lambda b, i: (b, i, 0))
    return pl.pallas_call(
        _out_proj_kernel,
        grid=(bsz, seq // tm),
        in_specs=[row, row, pl.BlockSpec((None, 1, d), lambda b, i: (b, 0, 0)),
                  _resident((d, d), lambda b, i: (0, 0))],
        out_specs=row,
        out_shape=jax.ShapeDtypeStruct(x.shape, F32),
        compiler_params=_params(2),
        name="out_proj",
    )(att, x, gate, w_out)


def _head_gain(gain, scale):
    return (jnp.tile(gain.astype(F32), LANES // HEAD_DIM) * scale).reshape(1, LANES)


def kernel(x, c, ada_w, ada_b, norm_g, ffn_w13, ffn_w2, rel_bias, swa_w_in, swa_w_out,
           swa_q_g, swa_k_g, swa_sink, fox_w_in, fox_w_out, fox_b_f, fox_q_g, fox_k_g):
    bsz = x.shape[0]
    mod = _ada(c, ada_w, ada_b).reshape(DEPTH, bsz, 3, 3, 1, D_MODEL)
    q_scale = HEAD_DIM ** -0.5
    bias_tiles = _swa_bias_tiles(rel_bias)

    def chunked_ffn(layer, half):
        w13 = ffn_w13[layer, half].astype(BF16)
        wg = w13[:, :D_FF].reshape(D_MODEL, N_FF_CHUNKS, FF_CHUNK).transpose(1, 0, 2)
        wu = w13[:, D_FF:].reshape(D_MODEL, N_FF_CHUNKS, FF_CHUNK).transpose(1, 0, 2)
        w2 = ffn_w2[layer, half].astype(BF16).reshape(N_FF_CHUNKS, FF_CHUNK, D_MODEL)
        return wg, wu, w2

    for layer in range(DEPTH):
        g = norm_g[layer].reshape(3, 1, D_MODEL)
        m = mod[layer]
        x = _ffn(x, m[:, 0, 0], m[:, 0, 1], m[:, 0, 2], g[0], *chunked_ffn(layer, 0))
        j = layer // 2
        if layer % 2 == 0:
            q, k2, v2 = _swa_proj(x, m[:, 1, 0], m[:, 1, 1], g[1], swa_w_in[j].astype(BF16),
                                  _head_gain(swa_q_g[j], q_scale), _head_gain(swa_k_g[j], 1.0))
            x = _swa_attn(swa_sink[j].astype(F32), q, k2, v2, bias_tiles, x, m[:, 1, 2],
                          swa_w_out[j].astype(BF16))
        else:
            w_in = jnp.pad(fox_w_in[j], ((0, 0), (0, LANES - N_HEADS))).astype(BF16)
            b_f = jnp.pad(fox_b_f[j].astype(F32), (0, LANES - N_HEADS)).reshape(1, LANES)
            q, k, v, ft = _fox_proj(x, m[:, 1, 0], m[:, 1, 1], g[1], w_in, b_f,
                                    _head_gain(fox_q_g[j], q_scale * LOG2E),
                                    _head_gain(fox_k_g[j], 1.0))
            att = _fox_attn(q, k, v, ft)
            x = _out_proj(att, x, m[:, 1, 2], fox_w_out[j].astype(BF16))
        x = _ffn(x, m[:, 2, 0], m[:, 2, 1], m[:, 2, 2], g[2], *chunked_ffn(layer, 1))
    return x
```

```python
import functools
import math

import numpy as np
import jax
import jax.numpy as jnp
from jax import lax
from jax.experimental import pallas as pl
from jax.experimental.pallas import tpu as pltpu

D_MODEL = 1024
DEPTH = 4
HEAD_DIM = 64
N_HEADS = D_MODEL // HEAD_DIM
A_KV_HEADS = 4
A_GROUP = N_HEADS // A_KV_HEADS
WINDOW = 128
BLOCK = 128
REL_BUCKETS = 32
REL_MAX_DIST = 128
D_FF = 2816
EPS = 1e-6

LANES = 128
FF_CHUNK = 256
N_FF_CHUNKS = D_FF // FF_CHUNK
ROW_TILE = 512
ATT_Q_TILE = 1024
ATT_K_TILE = 512
ATT_COL_TILE = 256
ONES_ROWS = 16
LOG2E = 1.4426950408889634
NEG = -1e30
VMEM_LIMIT = 56 * 1024 * 1024

BF16 = jnp.bfloat16
F32 = jnp.float32


def _params(n_axes):
    return pltpu.CompilerParams(
        dimension_semantics=("arbitrary",) * n_axes, vmem_limit_bytes=VMEM_LIMIT)


def _resident(shape, index_map):
    return pl.BlockSpec(shape, index_map, pipeline_mode=pl.Buffered(1))


def _modulated(x, g, shift, scale):
    ms = jnp.mean(x * x, axis=-1, keepdims=True)
    return (x * lax.rsqrt(ms + EPS) * g) * (1.0 + scale) + shift


def _ada_kernel(c_ref, w_ref, b_ref, o_ref):
    c = c_ref[...]
    c_act = c * jax.nn.sigmoid(c)
    o_ref[...] = jnp.dot(c_act, w_ref[...], precision=lax.Precision.HIGHEST,
                         preferred_element_type=F32) + b_ref[...]


def _ada(c, ada_w, ada_b):
    bsz = c.shape[0]
    n = ada_w.shape[-1]
    tn = D_MODEL
    return pl.pallas_call(
        _ada_kernel,
        grid=(DEPTH, n // tn),
        in_specs=[
            pl.BlockSpec((bsz, D_MODEL), lambda l, j: (0, 0)),
            pl.BlockSpec((None, D_MODEL, tn), lambda l, j: (l, 0, j)),
            pl.BlockSpec((None, 1, tn), lambda l, j: (l, 0, j)),
        ],
        out_specs=pl.BlockSpec((None, bsz, tn), lambda l, j: (l, 0, j)),
        out_shape=jax.ShapeDtypeStruct((DEPTH, bsz, n), F32),
        compiler_params=_params(2),
        name="ada_mod",
    )(c, ada_w, ada_b.reshape(DEPTH, 1, n))


def _ffn_kernel(x_ref, shift_ref, scale_ref, gate_ref, g_ref, w13_ref, w2_ref,
                o_ref, act_ref):
    x = x_ref[...]
    hb = _modulated(x, g_ref[...], shift_ref[...], scale_ref[...]).astype(BF16)
    for c in range(N_FF_CHUNKS):
        gate = jnp.dot(hb, w13_ref[:, c * FF_CHUNK:(c + 1) * FF_CHUNK],
                       preferred_element_type=F32)
        up = jnp.dot(hb, w13_ref[:, D_FF + c * FF_CHUNK:D_FF + (c + 1) * FF_CHUNK],
                     preferred_element_type=F32)
        act_ref[:, c * FF_CHUNK:(c + 1) * FF_CHUNK] = (
            gate * jax.nn.sigmoid(gate) * up).astype(BF16)
    y = jnp.dot(act_ref[...], w2_ref[...], preferred_element_type=F32)
    o_ref[...] = x + (0.5 * gate_ref[...]) * y


def _ffn(x, shift, scale, gate, g, w13, w2):
    bsz, seq, d = x.shape
    tm = ROW_TILE
    row = pl.BlockSpec((None, tm, d), lambda b, i: (b, i, 0))
    vec = pl.BlockSpec((None, 1, d), lambda b, i: (b, 0, 0))
    return pl.pallas_call(
        _ffn_kernel,
        grid=(bsz, seq // tm),
        in_specs=[
            row, vec, vec, vec,
            pl.BlockSpec((1, d), lambda b, i: (0, 0)),
            _resident((d, 2 * D_FF), lambda b, i: (0, 0)),
            _resident((D_FF, d), lambda b, i: (0, 0)),
        ],
        out_specs=row,
        out_shape=jax.ShapeDtypeStruct(x.shape, F32),
        scratch_shapes=[pltpu.VMEM((tm, D_FF), BF16)],
        compiler_params=_params(2),
        name="ffn_half_step",
    )(x, shift, scale, gate, g, w13, w2)


def _head_norm(t, gain):
    lo = lax.broadcasted_iota(jnp.int32, t.shape, 1) < HEAD_DIM
    t2 = t * t
    ss_lo = jnp.sum(jnp.where(lo, t2, 0.0), axis=-1, keepdims=True)
    ss_hi = jnp.sum(jnp.where(lo, 0.0, t2), axis=-1, keepdims=True)
    rs = jnp.where(lo, lax.rsqrt(ss_lo * (1.0 / HEAD_DIM) + EPS),
                   lax.rsqrt(ss_hi * (1.0 / HEAD_DIM) + EPS))
    return t * rs * gain


def _proj_chunks(hb, w_ref, col0, n_chunks):
    for c in range(0, n_chunks, 2):
        width = min(2, n_chunks - c) * LANES
        t = jnp.dot(hb, w_ref[:, col0 + c * LANES:col0 + c * LANES + width],
                    preferred_element_type=F32)
        for i in range(width // LANES):
            yield c + i, t[:, i * LANES:(i + 1) * LANES]


def _swa_proj_kernel(x_ref, shift_ref, scale_ref, g_ref, w_ref, qg_ref, kg_ref,
                     q_ref, k_ref, v_ref):
    hb = _modulated(x_ref[...], g_ref[...], shift_ref[...], scale_ref[...]).astype(BF16)
    lo = lax.broadcasted_iota(jnp.int32, (hb.shape[0], LANES), 1) < HEAD_DIM
    for c, t in _proj_chunks(hb, w_ref, 0, D_MODEL // LANES):
        q_ref[:, c * LANES:(c + 1) * LANES] = _head_norm(t, qg_ref[...]).astype(BF16)
    kv_cols = A_KV_HEADS * HEAD_DIM

    def store_duplicated(ref, c, t):
        r = pltpu.roll(t, HEAD_DIM, axis=1)
        ref[:, (2 * c) * LANES:(2 * c + 1) * LANES] = jnp.where(lo, t, r).astype(BF16)
        ref[:, (2 * c + 1) * LANES:(2 * c + 2) * LANES] = jnp.where(lo, r, t).astype(BF16)

    for c, t in _proj_chunks(hb, w_ref, D_MODEL, kv_cols // LANES):
        store_duplicated(k_ref, c, _head_norm(t, kg_ref[...]))
    for c, t in _proj_chunks(hb, w_ref, D_MODEL + kv_cols, kv_cols // LANES):
        store_duplicated(v_ref, c, t)


def _swa_proj(x, shift, scale, g, w_in, qg, kg):
    bsz, seq, d = x.shape
    tm = ROW_TILE
    n_in = w_in.shape[1]
    kvw = A_KV_HEADS * LANES
    row = lambda n: pl.BlockSpec((None, tm, n), lambda b, i: (b, i, 0))
    vec = pl.BlockSpec((None, 1, d), lambda b, i: (b, 0, 0))
    const = lambda n: pl.BlockSpec((1, n), lambda b, i: (0, 0))
    return pl.pallas_call(
        _swa_proj_kernel,
        grid=(bsz, seq // tm),
        in_specs=[row(d), vec, vec, const(d),
                  _resident((d, n_in), lambda b, i: (0, 0)),
                  const(LANES), const(LANES)],
        out_specs=[row(d), row(kvw), row(kvw)],
        out_shape=[jax.ShapeDtypeStruct((bsz, seq, d), BF16),
                   jax.ShapeDtypeStruct((bsz, seq, kvw), BF16),
                   jax.ShapeDtypeStruct((bsz, seq, kvw), BF16)],
        compiler_params=_params(2),
        name="swa_proj",
    )(x, shift, scale, g, w_in, qg, kg)


def _swa_attn_kernel(sink_ref, q_ref, k_ref, v_ref, bias_ref, x_ref, gate_ref, wo_ref,
                     o_ref, att_ref):
    tq = q_ref.shape[0]
    q0 = pl.program_id(1) * tq
    lo = lax.broadcasted_iota(jnp.int32, (BLOCK, LANES), 1) < HEAD_DIM
    for sb in range(tq // BLOCK):
        row0 = q0 + sb * BLOCK
        kstart = pl.multiple_of(jnp.maximum(row0 - BLOCK, 0), BLOCK)
        first = (row0 == 0).astype(jnp.int32)
        rows = slice(sb * BLOCK, (sb + 1) * BLOCK)
        for j in range(A_KV_HEADS):
            kc = k_ref[pl.ds(kstart, 2 * BLOCK), j * LANES:(j + 1) * LANES]
            vc = v_ref[pl.ds(kstart, 2 * BLOCK), j * LANES:(j + 1) * LANES]
            qs, sinks = [], []
            for gi in range(A_GROUP):
                h = j * A_GROUP + gi
                qc = q_ref[rows, (h // 2) * LANES:(h // 2 + 1) * LANES]
                keep = lo if h % 2 == 0 else jnp.logical_not(lo)
                qs.append(jnp.where(keep, qc, jnp.zeros_like(qc)))
                sinks.append(jnp.full((BLOCK, 1), sink_ref[h], F32))
            qst = jnp.concatenate(qs, axis=0)
            sink = jnp.concatenate(sinks, axis=0)
            s = lax.dot_general(qst, kc, (((1,), (1,)), ((), ())),
                                preferred_element_type=F32)
            s = s + bias_ref[first, j]
            m = jnp.maximum(jnp.max(s, axis=-1, keepdims=True), sink)
            p = jnp.exp(s - m)
            denom = jnp.sum(p, axis=-1, keepdims=True) + jnp.exp(sink - m)
            o = jnp.dot(p.astype(BF16), vc, preferred_element_type=F32) / denom
            for half in range(2):
                even = o[(2 * half) * BLOCK:(2 * half + 1) * BLOCK]
                odd = o[(2 * half + 1) * BLOCK:(2 * half + 2) * BLOCK]
                ch = 2 * j + half
                att_ref[rows, ch * LANES:(ch + 1) * LANES] = (
                    jnp.where(lo, even, odd).astype(BF16))
    y = jnp.dot(att_ref[...], wo_ref[...], preferred_element_type=F32)
    o_ref[...] = x_ref[...] + gate_ref[...] * y


def _swa_attn(sink, q, k2, v2, bias, x, gate, w_out):
    bsz, seq, d = x.shape
    tq = ROW_TILE
    kvw = k2.shape[-1]
    row = lambda n: pl.BlockSpec((None, tq, n), lambda b, i: (b, i, 0))
    return pl.pallas_call(
        _swa_attn_kernel,
        grid=(bsz, seq // tq),
        in_specs=[
            pl.BlockSpec(memory_space=pltpu.SMEM),
            row(d),
            _resident((None, seq, kvw), lambda b, i: (b, 0, 0)),
            _resident((None, seq, kvw), lambda b, i: (b, 0, 0)),
            _resident(bias.shape, lambda b, i: (0, 0, 0, 0)),
            row(d),
            pl.BlockSpec((None, 1, d), lambda b, i: (b, 0, 0)),
            _resident((d, d), lambda b, i: (0, 0)),
        ],
        out_specs=row(d),
        out_shape=jax.ShapeDtypeStruct(x.shape, F32),
        scratch_shapes=[pltpu.VMEM((tq, d), BF16)],
        compiler_params=_params(2),
        name="swa_attn_out",
    )(sink, q, k2, v2, bias, x, gate, w_out)


def _rel_bucket_table():
    n = np.arange(WINDOW)
    max_exact = REL_BUCKETS // 2
    nf = np.maximum(n, 1).astype(np.float32)
    large = max_exact + (np.log(nf / max_exact) / math.log(REL_MAX_DIST / max_exact)
                         * (REL_BUCKETS - max_exact)).astype(np.int32)
    large = np.minimum(large, REL_BUCKETS - 1)
    return np.where(n < max_exact, n, large)


def _swa_bias_tiles(rel_bias):
    bucket = _rel_bucket_table()
    qi = np.arange(BLOCK)[:, None]
    kj = np.arange(2 * BLOCK)[None, :]
    tiles = []
    for key_offset in (BLOCK, 0):
        dist = qi + key_offset - kj
        valid = (dist >= 0) & (dist < WINDOW)
        idx = bucket[np.clip(dist, 0, WINDOW - 1)]
        b = jnp.transpose(rel_bias.astype(F32)[idx], (2, 0, 1))
        b = jnp.where(valid[None], b, NEG)
        tiles.append(b.reshape(A_KV_HEADS, A_GROUP * BLOCK, 2 * BLOCK))
    return jnp.stack(tiles)


def _fox_proj_kernel(x_ref, shift_ref, scale_ref, g_ref, w_ref, bf_ref, qg_ref, kg_ref,
                     q_ref, k_ref, v_ref, ft_ref, carry_ref):
    tm = x_ref.shape[0]
    hb = _modulated(x_ref[...], g_ref[...], shift_ref[...], scale_ref[...]).astype(BF16)
    n_c = D_MODEL // LANES
    for c, t in _proj_chunks(hb, w_ref, 0, n_c):
        q_ref[:, c * LANES:(c + 1) * LANES] = _head_norm(t, qg_ref[...]).astype(BF16)
    for c, t in _proj_chunks(hb, w_ref, D_MODEL, n_c):
        k_ref[:, c * LANES:(c + 1) * LANES] = _head_norm(t, kg_ref[...]).astype(BF16)
    for c, t in _proj_chunks(hb, w_ref, 2 * D_MODEL, n_c):
        v_ref[:, c * LANES:(c + 1) * LANES] = t.astype(BF16)

    z = jnp.dot(hb, w_ref[:, 3 * D_MODEL:3 * D_MODEL + LANES],
                preferred_element_type=F32) + bf_ref[...]
    lane = lax.broadcasted_iota(jnp.int32, (tm, LANES), 1)
    log_f = jnp.where(lane < N_HEADS,
                      jnp.minimum(z, 0.0) - jnp.log1p(jnp.exp(-jnp.abs(z))), 0.0)
    hi = log_f.astype(BF16).astype(F32)
    rest = log_f - hi
    mid = rest.astype(BF16).astype(F32)
    low = (rest - mid).astype(BF16).astype(F32)
    packed = (hi + pltpu.roll(mid, N_HEADS, axis=1)
              + pltpu.roll(low, 2 * N_HEADS, axis=1)).astype(BF16)
    r = lax.broadcasted_iota(jnp.int32, (tm, tm), 0)
    cidx = lax.broadcasted_iota(jnp.int32, (tm, tm), 1)
    tri = (cidx <= r).astype(BF16)
    part = jnp.dot(tri, packed, preferred_element_type=F32)

    @pl.when(pl.program_id(1) == 0)
    def _():
        carry_ref[...] = jnp.zeros_like(carry_ref)

    cum = (part + pltpu.roll(part, LANES - N_HEADS, axis=1)
           + pltpu.roll(part, LANES - 2 * N_HEADS, axis=1)) + carry_ref[...]
    carry_ref[...] = cum[tm - 1:tm, :]
    ft_ref[...] = cum.T[:N_HEADS, :]


def _fox_proj(x, shift, scale, g, w_in, b_f, qg, kg):
    bsz, seq, d = x.shape
    tm = ROW_TILE
    n_in = w_in.shape[1]
    row = pl.BlockSpec((None, tm, d), lambda b, i: (b, i, 0))
    vec = pl.BlockSpec((None, 1, d), lambda b, i: (b, 0, 0))
    const = lambda n: pl.BlockSpec((1, n), lambda b, i: (0, 0))
    act = jax.ShapeDtypeStruct((bsz, seq, d), BF16)
    return pl.pallas_call(
        _fox_proj_kernel,
        grid=(bsz, seq // tm),
        in_specs=[row, vec, vec, const(d),
                  _resident((d, n_in), lambda b, i: (0, 0)),
                  const(LANES), const(LANES), const(LANES)],
        out_specs=[row, row, row,
                   pl.BlockSpec((None, N_HEADS, tm), lambda b, i: (b, 0, i))],
        out_shape=[act, act, act, jax.ShapeDtypeStruct((bsz, N_HEADS, seq), F32)],
        scratch_shapes=[pltpu.VMEM((1, LANES), F32)],
        compiler_params=_params(2),
        name="fox_proj",
    )(x, shift, scale, g, w_in, b_f, qg, kg)


def _fox_attn_kernel(q_ref, k_ref, v_ref, f_ref, o_ref, vt_ref, wq_ref,
                     s00_ref, s01_ref, s10_ref, s11_ref, p00_ref, p01_ref, p10_ref, p11_ref,
                     m_ref, alpha_ref, acc_ref):
    s_refs = ((s00_ref, s01_ref), (s10_ref, s11_ref))
    p_refs = ((p00_ref, p01_ref), (p10_ref, p11_ref))
    tq = q_ref.shape[0]
    tk = ATT_K_TILE
    seq = k_ref.shape[0]
    qi = pl.program_id(2)

    @pl.when(qi == 0)
    def _():
        for c in range(seq // tk):
            blk = v_ref[c * tk:(c + 1) * tk, :].astype(F32)
            vt_ref[:, c * tk:(c + 1) * tk] = blk.T.astype(BF16)

    q0 = pl.multiple_of(qi * tq, tq)
    qt = q_ref[...].astype(F32).T.astype(BF16)
    zeros = jnp.zeros((HEAD_DIM, tq), BF16)
    wq_ref[0] = jnp.concatenate([qt[:HEAD_DIM], zeros], axis=0)
    wq_ref[1] = jnp.concatenate([zeros, qt[HEAD_DIM:]], axis=0)
    ones = jnp.ones((ONES_ROWS, tk), BF16)
    m_ref[...] = jnp.full_like(m_ref, NEG)
    alpha_ref[...] = jnp.zeros_like(alpha_ref)
    acc_ref[...] = jnp.zeros_like(acc_ref)
    for a in range(2):
        p_refs[1][a][...] = jnp.zeros_like(p_refs[1][a])

    n_ct = tq // ATT_COL_TILE

    def cols(ct):
        return slice(ct * ATT_COL_TILE, (ct + 1) * ATT_COL_TILE)

    def key_bias(j):
        k0 = pl.multiple_of(j * tk, tk)
        out = []
        for a in range(2):
            f_rel = (f_ref[a:a + 1, pl.ds(k0, tk)]
                     - f_ref[a:a + 1, pl.ds(q0, LANES)][:, :1]) * LOG2E
            f_col = jnp.broadcast_to(f_rel, (LANES, tk)).T
            out.append(jnp.concatenate([f_col] * (ATT_COL_TILE // LANES), axis=1))
        return out

    def score_stage(j, slot, ct, bias):
        kc = k_ref[pl.ds(pl.multiple_of(j * tk, tk), tk), :]
        for a in range(2):
            s_refs[slot][a][:, cols(ct)] = (
                jnp.dot(kc, wq_ref[a, :, cols(ct)], preferred_element_type=F32) - bias[a])

    def softmax_stage(j, slot, ct, masked):
        k0 = pl.multiple_of(j * tk, tk)
        for a in range(2):
            s = s_refs[slot][a][:, cols(ct)]
            if masked:
                key = k0 + lax.broadcasted_iota(jnp.int32, s.shape, 0)
                qry = (q0 + ct * ATT_COL_TILE) + lax.broadcasted_iota(jnp.int32, s.shape, 1)
                s = jnp.where(key <= qry, s, NEG)
                s_refs[slot][a][:, cols(ct)] = s
            m_old = m_ref[a, :, cols(ct)]
            m_new = jnp.maximum(m_old, jnp.max(s, axis=0, keepdims=True))
            alpha_ref[a, :, cols(ct)] = jnp.exp2(m_old - m_new)
            p_refs[slot][a][:, cols(ct)] = jnp.exp2(
                s_refs[slot][a][:, cols(ct)] - m_new).astype(BF16)
            m_ref[a, :, cols(ct)] = m_new

    def pv_stage(j, slot, ct):
        k0 = pl.multiple_of(jnp.maximum(j, 0) * tk, tk)
        for a in range(2):
            v_aug = jnp.concatenate(
                [vt_ref[a * HEAD_DIM:(a + 1) * HEAD_DIM, pl.ds(k0, tk)], ones], axis=0)
            acc_ref[a, :, cols(ct)] = (
                alpha_ref[a, :, cols(ct)] * acc_ref[a, :, cols(ct)]
                + jnp.dot(v_aug, p_refs[slot][a][:, cols(ct)], preferred_element_type=F32))

    n_diag = tq // tk
    n_full = qi * n_diag

    def visibility(d, ct):
        if d < 0 or d >= n_diag:
            return "full" if d < 0 else "skip"
        first_key, last_key = d * tk, (d + 1) * tk - 1
        first_qry, last_qry = ct * ATT_COL_TILE, (ct + 1) * ATT_COL_TILE - 1
        if last_key <= first_qry:
            return "full"
        return "skip" if first_key > last_qry else "mask"

    def block_step(j, slot, d=-1):
        nxt = -1 if d < 0 else d + 1
        bias = key_bias(j + 1) if nxt < n_diag else None
        for ct in range(n_ct):
            if visibility(d - 1, ct) != "skip":
                pv_stage(j - 1, 1 - slot, ct)
            if visibility(nxt, ct) != "skip":
                score_stage(j + 1, 1 - slot, ct, bias)
            if visibility(d, ct) != "skip":
                softmax_stage(j, slot, ct, visibility(d, ct) == "mask")

    bias0 = key_bias(0)
    for ct in range(n_ct):
        score_stage(0, 0, ct, bias0)

    def pair(ii, carry):
        block_step(2 * ii, 0)
        block_step(2 * ii + 1, 1)
        return carry

    lax.fori_loop(0, n_full // 2, pair, 0)
    for d in range(n_diag):
        block_step(n_full + d, d % 2, d)
    for ct in range(n_ct):
        if visibility(n_diag - 1, ct) != "skip":
            pv_stage(n_full + n_diag - 1, (n_diag - 1) % 2, ct)

    outs = []
    for a in range(2):
        acc = acc_ref[a]
        outs.append((acc[:HEAD_DIM] / acc[HEAD_DIM:HEAD_DIM + 1]).T)
    o_ref[...] = jnp.concatenate(outs, axis=1).astype(BF16)


def _fox_attn(q, k, v, ft):
    bsz, seq, d = q.shape
    tq, tk = ATT_Q_TILE, ATT_K_TILE
    n_pairs = d // LANES
    ft = ft.reshape(bsz, n_pairs, 2, seq)
    return pl.pallas_call(
        _fox_attn_kernel,
        grid=(bsz, n_pairs, seq // tq),
        in_specs=[
            pl.BlockSpec((None, tq, LANES), lambda b, h, i: (b, i, h)),
            pl.BlockSpec((None, seq, LANES), lambda b, h, i: (b, 0, h)),
            pl.BlockSpec((None, seq, LANES), lambda b, h, i: (b, 0, h)),
            pl.BlockSpec((None, None, 2, seq), lambda b, h, i: (b, h, 0, 0)),
        ],
        out_specs=pl.BlockSpec((None, tq, LANES), lambda b, h, i: (b, i, h)),
        out_shape=jax.ShapeDtypeStruct((bsz, seq, d), BF16),
        scratch_shapes=(
            [pltpu.VMEM((LANES, seq), BF16), pltpu.VMEM((2, LANES, tq), BF16)]
            + [pltpu.VMEM((tk, tq), F32)] * 4 + [pltpu.VMEM((tk, tq), BF16)] * 4
            + [pltpu.VMEM((2, 1, tq), F32), pltpu.VMEM((2, 1, tq), F32),
               pltpu.VMEM((2, HEAD_DIM + ONES_ROWS, tq), F32)]),
        compiler_params=_params(3),
        name="fox_attn",
    )(q, k, v, ft)


def _out_proj_kernel(a_ref, x_ref, gate_ref, wo_ref, o_ref):
    y = jnp.dot(a_ref[...], wo_ref[...], preferred_element_type=F32)
    o_ref[...] = x_ref[...] + gate_ref[...] * y


def _out_proj(att, x, gate, w_out):
    bsz, seq, d = x.shape
    tm = ROW_TILE
    row = pl.BlockSpec((None, tm, d), lambda b, i: (b, i, 0))
    return pl.pallas_call(
        _out_proj_kernel,
        grid=(bsz, seq // tm),
        in_specs=[row, row, pl.BlockSpec((None, 1, d), lambda b, i: (b, 0, 0)),
                  _resident((d, d), lambda b, i: (0, 0))],
        out_specs=row,
        out_shape=jax.ShapeDtypeStruct(x.shape, F32),
        compiler_params=_params(2),
        name="out_proj",
    )(att, x, gate, w_out)


def _head_gain(gain, scale):
    return (jnp.tile(gain.astype(F32), LANES // HEAD_DIM) * scale).reshape(1, LANES)


def kernel(x, c, ada_w, ada_b, norm_g, ffn_w13, ffn_w2, rel_bias, swa_w_in, swa_w_out,
           swa_q_g, swa_k_g, swa_sink, fox_w_in, fox_w_out, fox_b_f, fox_q_g, fox_k_g):
    bsz = x.shape[0]
    mod = _ada(c, ada_w, ada_b).reshape(DEPTH, bsz, 3, 3, 1, D_MODEL)
    q_scale = HEAD_DIM ** -0.5
    bias_tiles = _swa_bias_tiles(rel_bias)

    w13_b = ffn_w13.astype(BF16)
    w2_b = ffn_w2.astype(BF16)

    for layer in range(DEPTH):
        g = norm_g[layer].reshape(3, 1, D_MODEL)
        m = mod[layer]
        x = _ffn(x, m[:, 0, 0], m[:, 0, 1], m[:, 0, 2], g[0], w13_b[layer, 0], w2_b[layer, 0])
        j = layer // 2
        if layer % 2 == 0:
            q, k2, v2 = _swa_proj(x, m[:, 1, 0], m[:, 1, 1], g[1], swa_w_in[j].astype(BF16),
                                  _head_gain(swa_q_g[j], q_scale), _head_gain(swa_k_g[j], 1.0))
            x = _swa_attn(swa_sink[j].astype(F32), q, k2, v2, bias_tiles, x, m[:, 1, 2],
                          swa_w_out[j].astype(BF16))
        else:
            w_in = jnp.pad(fox_w_in[j], ((0, 0), (0, LANES - N_HEADS))).astype(BF16)
            b_f = jnp.pad(fox_b_f[j].astype(F32), (0, LANES - N_HEADS)).reshape(1, LANES)
            q, k, v, ft = _fox_proj(x, m[:, 1, 0], m[:, 1, 1], g[1], w_in, b_f,
                                    _head_gain(fox_q_g[j], q_scale * LOG2E),
                                    _head_gain(fox_k_g[j], 1.0))
            att = _fox_attn(q, k, v, ft)
            x = _out_proj(att, x, m[:, 1, 2], fox_w_out[j].astype(BF16))
        x = _ffn(x, m[:, 2, 0], m[:, 2, 1], m[:, 2, 2], g[2], w13_b[layer, 1], w2_b[layer, 1])
    return x
```

```python
import functools
import math

import numpy as np
import jax
import jax.numpy as jnp
from jax import lax
from jax.experimental import pallas as pl
from jax.experimental.pallas import tpu as pltpu

D_MODEL = 1024
DEPTH = 4
HEAD_DIM = 64
N_HEADS = D_MODEL // HEAD_DIM
A_KV_HEADS = 4
A_GROUP = N_HEADS // A_KV_HEADS
WINDOW = 128
BLOCK = 128
REL_BUCKETS = 32
REL_MAX_DIST = 128
D_FF = 2816
EPS = 1e-6

LANES = 128
FF_CHUNK = 256
N_FF_CHUNKS = D_FF // FF_CHUNK
ROW_TILE = 512
ATT_Q_TILE = 2048
ATT_K_TILE = 512
ATT_COL_TILE = 256
ONES_ROWS = 16
LOG2E = 1.4426950408889634
NEG = -1e30
VMEM_LIMIT = 56 * 1024 * 1024

BF16 = jnp.bfloat16
F32 = jnp.float32


def _params(n_axes):
    return pltpu.CompilerParams(
        dimension_semantics=("arbitrary",) * n_axes, vmem_limit_bytes=VMEM_LIMIT)


def _resident(shape, index_map):
    return pl.BlockSpec(shape, index_map, pipeline_mode=pl.Buffered(1))


def _modulated(x, g, shift, scale):
    ms = jnp.mean(x * x, axis=-1, keepdims=True)
    return (x * lax.rsqrt(ms + EPS) * g) * (1.0 + scale) + shift


def _ada_kernel(c_ref, w_ref, b_ref, o_ref):
    c = c_ref[...]
    c_act = c * jax.nn.sigmoid(c)
    o_ref[...] = jnp.dot(c_act, w_ref[...], precision=lax.Precision.HIGHEST,
                         preferred_element_type=F32) + b_ref[...]


def _ada(c, ada_w, ada_b):
    bsz = c.shape[0]
    n = ada_w.shape[-1]
    tn = D_MODEL
    return pl.pallas_call(
        _ada_kernel,
        grid=(DEPTH, n // tn),
        in_specs=[
            pl.BlockSpec((bsz, D_MODEL), lambda l, j: (0, 0)),
            pl.BlockSpec((None, D_MODEL, tn), lambda l, j: (l, 0, j)),
            pl.BlockSpec((None, 1, tn), lambda l, j: (l, 0, j)),
        ],
        out_specs=pl.BlockSpec((None, bsz, tn), lambda l, j: (l, 0, j)),
        out_shape=jax.ShapeDtypeStruct((DEPTH, bsz, n), F32),
        compiler_params=_params(2),
        name="ada_mod",
    )(c, ada_w, ada_b.reshape(DEPTH, 1, n))


def _ffn_kernel(x_ref, shift_ref, scale_ref, gate_ref, g_ref, w13_ref, w2_ref,
                o_ref, act_ref):
    x = x_ref[...]
    hb = _modulated(x, g_ref[...], shift_ref[...], scale_ref[...]).astype(BF16)
    for c in range(N_FF_CHUNKS):
        gate = jnp.dot(hb, w13_ref[:, c * FF_CHUNK:(c + 1) * FF_CHUNK],
                       preferred_element_type=F32)
        up = jnp.dot(hb, w13_ref[:, D_FF + c * FF_CHUNK:D_FF + (c + 1) * FF_CHUNK],
                     preferred_element_type=F32)
        act_ref[:, c * FF_CHUNK:(c + 1) * FF_CHUNK] = (
            gate * jax.nn.sigmoid(gate) * up).astype(BF16)
    y = jnp.dot(act_ref[...], w2_ref[...], preferred_element_type=F32)
    o_ref[...] = x + (0.5 * gate_ref[...]) * y


def _ffn(x, shift, scale, gate, g, w13, w2):
    bsz, seq, d = x.shape
    tm = ROW_TILE
    row = pl.BlockSpec((None, tm, d), lambda b, i: (b, i, 0))
    vec = pl.BlockSpec((None, 1, d), lambda b, i: (b, 0, 0))
    return pl.pallas_call(
        _ffn_kernel,
        grid=(bsz, seq // tm),
        in_specs=[
            row, vec, vec, vec,
            pl.BlockSpec((1, d), lambda b, i: (0, 0)),
            _resident((d, 2 * D_FF), lambda b, i: (0, 0)),
            _resident((D_FF, d), lambda b, i: (0, 0)),
        ],
        out_specs=row,
        out_shape=jax.ShapeDtypeStruct(x.shape, F32),
        scratch_shapes=[pltpu.VMEM((tm, D_FF), BF16)],
        compiler_params=_params(2),
        name="ffn_half_step",
    )(x, shift, scale, gate, g, w13, w2)


def _head_norm(t, gain):
    lo = lax.broadcasted_iota(jnp.int32, t.shape, 1) < HEAD_DIM
    t2 = t * t
    ss_lo = jnp.sum(jnp.where(lo, t2, 0.0), axis=-1, keepdims=True)
    ss_hi = jnp.sum(jnp.where(lo, 0.0, t2), axis=-1, keepdims=True)
    rs = jnp.where(lo, lax.rsqrt(ss_lo * (1.0 / HEAD_DIM) + EPS),
                   lax.rsqrt(ss_hi * (1.0 / HEAD_DIM) + EPS))
    return t * rs * gain


def _proj_chunks(hb, w_ref, col0, n_chunks):
    for c in range(0, n_chunks, 2):
        width = min(2, n_chunks - c) * LANES
        t = jnp.dot(hb, w_ref[:, col0 + c * LANES:col0 + c * LANES + width],
                    preferred_element_type=F32)
        for i in range(width // LANES):
            yield c + i, t[:, i * LANES:(i + 1) * LANES]


def _swa_proj_kernel(x_ref, shift_ref, scale_ref, g_ref, w_ref, qg_ref, kg_ref,
                     q_ref, k_ref, v_ref):
    hb = _modulated(x_ref[...], g_ref[...], shift_ref[...], scale_ref[...]).astype(BF16)
    lo = lax.broadcasted_iota(jnp.int32, (hb.shape[0], LANES), 1) < HEAD_DIM
    for c, t in _proj_chunks(hb, w_ref, 0, D_MODEL // LANES):
        q_ref[:, c * LANES:(c + 1) * LANES] = _head_norm(t, qg_ref[...]).astype(BF16)
    kv_cols = A_KV_HEADS * HEAD_DIM

    def store_duplicated(ref, c, t):
        r = pltpu.roll(t, HEAD_DIM, axis=1)
        ref[:, (2 * c) * LANES:(2 * c + 1) * LANES] = jnp.where(lo, t, r).astype(BF16)
        ref[:, (2 * c + 1) * LANES:(2 * c + 2) * LANES] = jnp.where(lo, r, t).astype(BF16)

    for c, t in _proj_chunks(hb, w_ref, D_MODEL, kv_cols // LANES):
        store_duplicated(k_ref, c, _head_norm(t, kg_ref[...]))
    for c, t in _proj_chunks(hb, w_ref, D_MODEL + kv_cols, kv_cols // LANES):
        v_ref[:, c * LANES:(c + 1) * LANES] = t.astype(BF16)


def _swa_proj(x, shift, scale, g, w_in, qg, kg):
    bsz, seq, d = x.shape
    tm = ROW_TILE
    n_in = w_in.shape[1]
    kvw = A_KV_HEADS * LANES
    kv_cols = A_KV_HEADS * HEAD_DIM
    row = lambda n: pl.BlockSpec((None, tm, n), lambda b, i: (b, i, 0))
    vec = pl.BlockSpec((None, 1, d), lambda b, i: (b, 0, 0))
    const = lambda n: pl.BlockSpec((1, n), lambda b, i: (0, 0))
    return pl.pallas_call(
        _swa_proj_kernel,
        grid=(bsz, seq // tm),
        in_specs=[row(d), vec, vec, const(d),
                  _resident((d, n_in), lambda b, i: (0, 0)),
                  const(LANES), const(LANES)],
        out_specs=[row(d), row(kvw), row(kv_cols)],
        out_shape=[jax.ShapeDtypeStruct((bsz, seq, d), BF16),
                   jax.ShapeDtypeStruct((bsz, seq, kvw), BF16),
                   jax.ShapeDtypeStruct((bsz, seq, kv_cols), BF16)],
        compiler_params=_params(2),
        name="swa_proj",
    )(x, shift, scale, g, w_in, qg, kg)


def _swa_attn_kernel(sink_ref, q_ref, k_ref, v_ref, bias_ref, x_ref, gate_ref, wo_ref,
                     o_ref, att_ref):
    tq = q_ref.shape[0]
    q0 = pl.program_id(1) * tq
    zeros = jnp.zeros((HEAD_DIM, BLOCK), BF16)
    ones = jnp.ones((ONES_ROWS, 2 * BLOCK), BF16)
    units = [(sb, j) for sb in range(tq // BLOCK) for j in range(A_KV_HEADS)]

    def window(sb):
        row0 = q0 + sb * BLOCK
        return pl.multiple_of(jnp.maximum(row0 - BLOCK, 0), BLOCK), (row0 == 0).astype(jnp.int32)

    def score_stage(sb, j):
        kstart, first = window(sb)
        rows = slice(sb * BLOCK, (sb + 1) * BLOCK)
        kc = k_ref[pl.ds(kstart, 2 * BLOCK), j * LANES:(j + 1) * LANES]
        wq = []
        for half in range(2):
            qt = q_ref[rows, (2 * j + half) * LANES:(2 * j + half + 1) * LANES]
            qt = qt.astype(F32).T.astype(BF16)
            wq.append(jnp.concatenate([qt[:HEAD_DIM], zeros], axis=0))
            wq.append(jnp.concatenate([zeros, qt[HEAD_DIM:]], axis=0))
        s = jnp.dot(kc, jnp.concatenate(wq, axis=1), preferred_element_type=F32)
        return s + bias_ref[first, j]

    def softmax_stage(j, s):
        sink = sink_ref[j]
        m = jnp.maximum(jnp.max(s, axis=0, keepdims=True), sink)
        return jnp.exp2(s - m).astype(BF16), jnp.exp2(sink - m)

    def pv_stage(sb, j, p, p_sink):
        kstart, _ = window(sb)
        rows = slice(sb * BLOCK, (sb + 1) * BLOCK)
        vt = v_ref[pl.ds(kstart, 2 * BLOCK), (j // 2) * LANES:(j // 2 + 1) * LANES]
        vt = vt.astype(F32).T.astype(BF16)
        v_aug = jnp.concatenate(
            [vt[(j % 2) * HEAD_DIM:(j % 2 + 1) * HEAD_DIM], ones], axis=0)
        acc = jnp.dot(v_aug, p, preferred_element_type=F32)
        ot = acc[:HEAD_DIM] / (acc[HEAD_DIM:HEAD_DIM + 1] + p_sink)
        for half in range(2):
            pair = jnp.concatenate(
                [ot[:, (2 * half) * BLOCK:(2 * half + 1) * BLOCK],
                 ot[:, (2 * half + 1) * BLOCK:(2 * half + 2) * BLOCK]], axis=0)
            ch = 2 * j + half
            att_ref[rows, ch * LANES:(ch + 1) * LANES] = pair.T.astype(BF16)

    scores = score_stage(*units[0])
    pending = None
    for u, (sb, j) in enumerate(units):
        nxt = score_stage(*units[u + 1]) if u + 1 < len(units) else None
        probs = softmax_stage(j, scores)
        if pending is not None:
            pv_stage(*pending)
        pending = (sb, j) + probs
        scores = nxt
    pv_stage(*pending)
    y = jnp.dot(att_ref[...], wo_ref[...], preferred_element_type=F32)
    o_ref[...] = x_ref[...] + gate_ref[...] * y


def _swa_attn(sink, q, k2, v, bias, x, gate, w_out):
    bsz, seq, d = x.shape
    tq = ROW_TILE
    kvw = k2.shape[-1]
    row = lambda n: pl.BlockSpec((None, tq, n), lambda b, i: (b, i, 0))
    return pl.pallas_call(
        _swa_attn_kernel,
        grid=(bsz, seq // tq),
        in_specs=[
            pl.BlockSpec(sink.shape, lambda b, i: (0, 0, 0)),
            row(d),
            _resident((None, seq, kvw), lambda b, i: (b, 0, 0)),
            _resident((None, seq, v.shape[-1]), lambda b, i: (b, 0, 0)),
            _resident(bias.shape, lambda b, i: (0, 0, 0, 0)),
            row(d),
            pl.BlockSpec((None, 1, d), lambda b, i: (b, 0, 0)),
            _resident((d, d), lambda b, i: (0, 0)),
        ],
        out_specs=row(d),
        out_shape=jax.ShapeDtypeStruct(x.shape, F32),
        scratch_shapes=[pltpu.VMEM((tq, d), BF16)],
        compiler_params=_params(2),
        name="swa_attn_out",
    )(sink, q, k2, v, bias, x, gate, w_out)


def _rel_bucket_table():
    n = np.arange(WINDOW)
    max_exact = REL_BUCKETS // 2
    nf = np.maximum(n, 1).astype(np.float32)
    large = max_exact + (np.log(nf / max_exact) / math.log(REL_MAX_DIST / max_exact)
                         * (REL_BUCKETS - max_exact)).astype(np.int32)
    large = np.minimum(large, REL_BUCKETS - 1)
    return np.where(n < max_exact, n, large)


def _swa_bias_tiles(rel_bias):
    by_dist = rel_bias.astype(F32)[_rel_bucket_table()].T
    rev = by_dist[:, ::-1]
    n_h = rev.shape[0]
    span = 3 * BLOCK - 1
    tiles = []
    for key_offset in (BLOCK, 0):
        neg = lambda n: jnp.full((n_h, n), NEG, F32)
        w = jnp.concatenate([neg(key_offset), rev, neg(span - key_offset - WINDOW)], axis=1)
        z = jnp.concatenate([w[:, BLOCK - 1:], neg(1), w[:, :BLOCK - 1]], axis=1)
        length = z.shape[1]
        t = jnp.tile(z, (1, BLOCK))[:, :BLOCK * (length - 1)]
        t = t.reshape(n_h, BLOCK, length - 1)[:, :, :2 * BLOCK]
        b = t.reshape(A_KV_HEADS, A_GROUP * BLOCK, 2 * BLOCK)
        tiles.append(jnp.transpose(b, (0, 2, 1)))
    return jnp.stack(tiles)


def _fox_proj_kernel(x_ref, shift_ref, scale_ref, g_ref, w_ref, bf_ref, qg_ref, kg_ref,
                     q_ref, k_ref, v_ref, ft_ref, carry_ref):
    tm = x_ref.shape[0]
    hb = _modulated(x_ref[...], g_ref[...], shift_ref[...], scale_ref[...]).astype(BF16)
    n_c = D_MODEL // LANES
    for c, t in _proj_chunks(hb, w_ref, 0, n_c):
        q_ref[:, c * LANES:(c + 1) * LANES] = _head_norm(t, qg_ref[...]).astype(BF16)
    for c, t in _proj_chunks(hb, w_ref, D_MODEL, n_c):
        k_ref[:, c * LANES:(c + 1) * LANES] = _head_norm(t, kg_ref[...]).astype(BF16)
    for c, t in _proj_chunks(hb, w_ref, 2 * D_MODEL, n_c):
        v_ref[:, c * LANES:(c + 1) * LANES] = t.astype(BF16)

    z = jnp.dot(hb, w_ref[:, 3 * D_MODEL:3 * D_MODEL + LANES],
                preferred_element_type=F32) + bf_ref[...]
    lane = lax.broadcasted_iota(jnp.int32, (tm, LANES), 1)
    log_f = jnp.where(lane < N_HEADS,
                      jnp.minimum(z, 0.0) - jnp.log1p(jnp.exp(-jnp.abs(z))), 0.0)
    hi = log_f.astype(BF16).astype(F32)
    rest = log_f - hi
    mid = rest.astype(BF16).astype(F32)
    low = (rest - mid).astype(BF16).astype(F32)
    packed = (hi + pltpu.roll(mid, N_HEADS, axis=1)
              + pltpu.roll(low, 2 * N_HEADS, axis=1)).astype(BF16)
    r = lax.broadcasted_iota(jnp.int32, (tm, tm), 0)
    cidx = lax.broadcasted_iota(jnp.int32, (tm, tm), 1)
    tri = (cidx <= r).astype(BF16)
    part = jnp.dot(tri, packed, preferred_element_type=F32)

    @pl.when(pl.program_id(1) == 0)
    def _():
        carry_ref[...] = jnp.zeros_like(carry_ref)

    cum = (part + pltpu.roll(part, LANES - N_HEADS, axis=1)
           + pltpu.roll(part, LANES - 2 * N_HEADS, axis=1)) + carry_ref[...]
    carry_ref[...] = cum[tm - 1:tm, :]
    ft_ref[...] = cum.T[:N_HEADS, :]


def _fox_proj(x, shift, scale, g, w_in, b_f, qg, kg):
    bsz, seq, d = x.shape
    tm = ROW_TILE
    n_in = w_in.shape[1]
    row = pl.BlockSpec((None, tm, d), lambda b, i: (b, i, 0))
    vec = pl.BlockSpec((None, 1, d), lambda b, i: (b, 0, 0))
    const = lambda n: pl.BlockSpec((1, n), lambda b, i: (0, 0))
    act = jax.ShapeDtypeStruct((bsz, seq, d), BF16)
    return pl.pallas_call(
        _fox_proj_kernel,
        grid=(bsz, seq // tm),
        in_specs=[row, vec, vec, const(d),
                  _resident((d, n_in), lambda b, i: (0, 0)),
                  const(LANES), const(LANES), const(LANES)],
        out_specs=[row, row, row,
                   pl.BlockSpec((None, N_HEADS, tm), lambda b, i: (b, 0, i))],
        out_shape=[act, act, act, jax.ShapeDtypeStruct((bsz, N_HEADS, seq), F32)],
        scratch_shapes=[pltpu.VMEM((1, LANES), F32)],
        compiler_params=_params(2),
        name="fox_proj",
    )(x, shift, scale, g, w_in, b_f, qg, kg)


def _fox_attn_kernel(q_ref, k_ref, v_ref, f_ref, o_ref, vt_ref, wq_ref,
                     s00_ref, s01_ref, s10_ref, s11_ref, p00_ref, p01_ref, p10_ref, p11_ref,
                     cmax_ref, m_ref, alpha_ref, acc_ref):
    s_refs = ((s00_ref, s01_ref), (s10_ref, s11_ref))
    p_refs = ((p00_ref, p01_ref), (p10_ref, p11_ref))
    tq = q_ref.shape[0]
    tk = ATT_K_TILE
    seq = k_ref.shape[0]
    qi = pl.program_id(2)

    @pl.when(qi == 0)
    def _():
        for c in range(seq // tk):
            blk = v_ref[c * tk:(c + 1) * tk, :].astype(F32)
            vt_ref[:, c * tk:(c + 1) * tk] = blk.T.astype(BF16)

    q0 = pl.multiple_of(qi * tq, tq)
    qt = q_ref[...].astype(F32).T.astype(BF16)
    zeros = jnp.zeros((HEAD_DIM, tq), BF16)
    wq_ref[0] = jnp.concatenate([qt[:HEAD_DIM], zeros], axis=0)
    wq_ref[1] = jnp.concatenate([zeros, qt[HEAD_DIM:]], axis=0)
    ones = jnp.ones((ONES_ROWS, tk), BF16)
    m_ref[...] = jnp.full_like(m_ref, NEG)
    alpha_ref[...] = jnp.zeros_like(alpha_ref)
    acc_ref[...] = jnp.zeros_like(acc_ref)
    for a in range(2):
        p_refs[1][a][...] = jnp.zeros_like(p_refs[1][a])

    n_ct = tq // ATT_COL_TILE

    def cols(ct):
        return slice(ct * ATT_COL_TILE, (ct + 1) * ATT_COL_TILE)

    def key_bias(j):
        k0 = pl.multiple_of(j * tk, tk)
        out = []
        for a in range(2):
            f_rel = (f_ref[a:a + 1, pl.ds(k0, tk)]
                     - f_ref[a:a + 1, pl.ds(q0, LANES)][:, :1]) * LOG2E
            f_col = jnp.broadcast_to(f_rel, (LANES, tk)).T
            out.append(jnp.concatenate([f_col] * (ATT_COL_TILE // LANES), axis=1))
        return out

    def score_stage(j, slot, ct, bias):
        kc = k_ref[pl.ds(pl.multiple_of(j * tk, tk), tk), :]
        for a in range(2):
            s = jnp.dot(kc, wq_ref[a, :, cols(ct)], preferred_element_type=F32) - bias[a]
            s_refs[slot][a][:, cols(ct)] = s
            cmax_ref[slot, a, :, cols(ct)] = jnp.max(s, axis=0, keepdims=True)

    def softmax_stage(j, slot, ct, masked):
        k0 = pl.multiple_of(j * tk, tk)
        for a in range(2):
            if masked:
                s = s_refs[slot][a][:, cols(ct)]
                key = k0 + lax.broadcasted_iota(jnp.int32, s.shape, 0)
                qry = (q0 + ct * ATT_COL_TILE) + lax.broadcasted_iota(jnp.int32, s.shape, 1)
                s = jnp.where(key <= qry, s, NEG)
                s_refs[slot][a][:, cols(ct)] = s
                col_max = jnp.max(s, axis=0, keepdims=True)
            else:
                col_max = cmax_ref[slot, a, :, cols(ct)]
            m_old = m_ref[a, :, cols(ct)]
            m_new = jnp.maximum(m_old, col_max)
            alpha_ref[a, :, cols(ct)] = jnp.exp2(m_old - m_new)
            p_refs[slot][a][:, cols(ct)] = jnp.exp2(
                s_refs[slot][a][:, cols(ct)] - m_new).astype(BF16)
            m_ref[a, :, cols(ct)] = m_new

    def pv_stage(j, slot, ct):
        k0 = pl.multiple_of(jnp.maximum(j, 0) * tk, tk)
        for a in range(2):
            v_aug = jnp.concatenate(
                [vt_ref[a * HEAD_DIM:(a + 1) * HEAD_DIM, pl.ds(k0, tk)], ones], axis=0)
            acc_ref[a, :, cols(ct)] = (
                alpha_ref[a, :, cols(ct)] * acc_ref[a, :, cols(ct)]
                + jnp.dot(v_aug, p_refs[slot][a][:, cols(ct)], preferred_element_type=F32))

    n_diag = tq // tk
    n_full = qi * n_diag

    def visibility(d, ct):
        if d < 0 or d >= n_diag:
            return "full" if d < 0 else "skip"
        first_key, last_key = d * tk, (d + 1) * tk - 1
        first_qry, last_qry = ct * ATT_COL_TILE, (ct + 1) * ATT_COL_TILE - 1
        if last_key <= first_qry:
            return "full"
        return "skip" if first_key > last_qry else "mask"

    def block_step(j, slot, d=-1):
        nxt = -1 if d < 0 else d + 1
        bias = key_bias(j + 1) if nxt < n_diag else None
        for ct in range(n_ct):
            if visibility(d - 1, ct) != "skip":
                pv_stage(j - 1, 1 - slot, ct)
            if visibility(nxt, ct) != "skip":
                score_stage(j + 1, 1 - slot, ct, bias)
            if visibility(d, ct) != "skip":
                softmax_stage(j, slot, ct, visibility(d, ct) == "mask")

    bias0 = key_bias(0)
    for ct in range(n_ct):
        score_stage(0, 0, ct, bias0)

    def pair(ii, carry):
        block_step(2 * ii, 0)
        block_step(2 * ii + 1, 1)
        return carry

    lax.fori_loop(0, n_full // 2, pair, 0)
    for d in range(n_diag):
        block_step(n_full + d, d % 2, d)
    for ct in range(n_ct):
        if visibility(n_diag - 1, ct) != "skip":
            pv_stage(n_full + n_diag - 1, (n_diag - 1) % 2, ct)

    outs = []
    for a in range(2):
        acc = acc_ref[a]
        outs.append((acc[:HEAD_DIM] / acc[HEAD_DIM:HEAD_DIM + 1]).T)
    o_ref[...] = jnp.concatenate(outs, axis=1).astype(BF16)


def _fox_attn(q, k, v, ft):
    bsz, seq, d = q.shape
    tq, tk = ATT_Q_TILE, ATT_K_TILE
    n_pairs = d // LANES
    ft = ft.reshape(bsz, n_pairs, 2, seq)
    return pl.pallas_call(
        _fox_attn_kernel,
        grid=(bsz, n_pairs, seq // tq),
        in_specs=[
            pl.BlockSpec((None, tq, LANES), lambda b, h, i: (b, i, h)),
            pl.BlockSpec((None, seq, LANES), lambda b, h, i: (b, 0, h)),
            pl.BlockSpec((None, seq, LANES), lambda b, h, i: (b, 0, h)),
            pl.BlockSpec((None, None, 2, seq), lambda b, h, i: (b, h, 0, 0)),
        ],
        out_specs=pl.BlockSpec((None, tq, LANES), lambda b, h, i: (b, i, h)),
        out_shape=jax.ShapeDtypeStruct((bsz, seq, d), BF16),
        scratch_shapes=(
            [pltpu.VMEM((LANES, seq), BF16), pltpu.VMEM((2, LANES, tq), BF16)]
            + [pltpu.VMEM((tk, tq), F32)] * 4 + [pltpu.VMEM((tk, tq), BF16)] * 4
            + [pltpu.VMEM((2, 2, 1, tq), F32),
               pltpu.VMEM((2, 1, tq), F32), pltpu.VMEM((2, 1, tq), F32),
               pltpu.VMEM((2, HEAD_DIM + ONES_ROWS, tq), F32)]),
        compiler_params=_params(3),
        name="fox_attn",
    )(q, k, v, ft)


def _out_proj_kernel(a_ref, x_ref, gate_ref, wo_ref, o_ref):
    y = jnp.dot(a_ref[...], wo_ref[...], preferred_element_type=F32)
    o_ref[...] = x_ref[...] + gate_ref[...] * y


def _out_proj(att, x, gate, w_out):
    bsz, seq, d = x.shape
    tm = ROW_TILE
    row = pl.BlockSpec((None, tm, d), lambda b, i: (b, i, 0))
    return pl.pallas_call(
        _out_proj_kernel,
        grid=(bsz, seq // tm),
        in_specs=[row, row, pl.BlockSpec((None, 1, d), lambda b, i: (b, 0, 0)),
                  _resident((d, d), lambda b, i: (0, 0))],
        out_specs=row,
        out_shape=jax.ShapeDtypeStruct(x.shape, F32),
        compiler_params=_params(2),
        name="out_proj",
    )(att, x, gate, w_out)


def _head_gain(gain, scale):
    return (jnp.tile(gain.astype(F32), LANES // HEAD_DIM) * scale).reshape(1, LANES)


def kernel(x, c, ada_w, ada_b, norm_g, ffn_w13, ffn_w2, rel_bias, swa_w_in, swa_w_out,
           swa_q_g, swa_k_g, swa_sink, fox_w_in, fox_w_out, fox_b_f, fox_q_g, fox_k_g):
    bsz = x.shape[0]
    mod = _ada(c, ada_w, ada_b).reshape(DEPTH, bsz, 3, 3, 1, D_MODEL)
    q_scale = HEAD_DIM ** -0.5
    bias_tiles = _swa_bias_tiles(rel_bias.astype(F32) * LOG2E)

    w13_b = ffn_w13.astype(BF16)
    w2_b = ffn_w2.astype(BF16)

    for layer in range(DEPTH):
        g = norm_g[layer].reshape(3, 1, D_MODEL)
        m = mod[layer]
        x = _ffn(x, m[:, 0, 0], m[:, 0, 1], m[:, 0, 2], g[0], w13_b[layer, 0], w2_b[layer, 0])
        j = layer // 2
        if layer % 2 == 0:
            q, k2, v2 = _swa_proj(x, m[:, 1, 0], m[:, 1, 1], g[1], swa_w_in[j].astype(BF16),
                                  _head_gain(swa_q_g[j], q_scale * LOG2E),
                                  _head_gain(swa_k_g[j], 1.0))
            sink = jnp.repeat((swa_sink[j].astype(F32) * LOG2E).reshape(A_KV_HEADS, A_GROUP), BLOCK,
                              axis=1).reshape(A_KV_HEADS, 1, A_GROUP * BLOCK)
            x = _swa_attn(sink, q, k2, v2, bias_tiles, x, m[:, 1, 2],
                          swa_w_out[j].astype(BF16))
        else:
            w_in = jnp.pad(fox_w_in[j], ((0, 0), (0, LANES - N_HEADS))).astype(BF16)
            b_f = jnp.pad(fox_b_f[j].astype(F32), (0, LANES - N_HEADS)).reshape(1, LANES)
            q, k, v, ft = _fox_proj(x, m[:, 1, 0], m[:, 1, 1], g[1], w_in, b_f,
                                    _head_gain(fox_q_g[j], q_scale * LOG2E),
                                    _head_gain(fox_k_g[j], 1.0))
            att = _fox_attn(q, k, v, ft)
            x = _out_proj(att, x, m[:, 1, 2], fox_w_out[j].astype(BF16))
        x = _ffn(x, m[:, 2, 0], m[:, 2, 1], m[:, 2, 2], g[2], w13_b[layer, 1], w2_b[layer, 1])
    return x
```

```python
import functools
import math

import numpy as np
import jax
import jax.numpy as jnp
from jax import lax
from jax.experimental import pallas as pl
from jax.experimental.pallas import tpu as pltpu

D_MODEL = 1024
DEPTH = 4
HEAD_DIM = 64
N_HEADS = D_MODEL // HEAD_DIM
A_KV_HEADS = 4
A_GROUP = N_HEADS // A_KV_HEADS
WINDOW = 128
BLOCK = 128
REL_BUCKETS = 32
REL_MAX_DIST = 128
D_FF = 2816
EPS = 1e-6

LANES = 128
FF_CHUNK = 256
N_FF_CHUNKS = D_FF // FF_CHUNK
ROW_TILE = 512
ATT_Q_TILE = 2048
ATT_K_TILE = 512
ATT_COL_TILE = 256
ONES_ROWS = 16
FORGET_TERMS = 3
LOG2E = 1.4426950408889634
NEG = -1e30
VMEM_LIMIT = 56 * 1024 * 1024

BF16 = jnp.bfloat16
F32 = jnp.float32


def _params(n_axes):
    return pltpu.CompilerParams(
        dimension_semantics=("arbitrary",) * n_axes, vmem_limit_bytes=VMEM_LIMIT)


def _resident(shape, index_map):
    return pl.BlockSpec(shape, index_map, pipeline_mode=pl.Buffered(1))


def _modulated(x, g, shift, scale):
    ms = jnp.mean(x * x, axis=-1, keepdims=True)
    return (x * lax.rsqrt(ms + EPS) * g) * (1.0 + scale) + shift


def _ada_kernel(c_ref, w_ref, b_ref, o_ref):
    c = c_ref[...]
    c_act = c * jax.nn.sigmoid(c)
    o_ref[...] = jnp.dot(c_act, w_ref[...], precision=lax.Precision.HIGHEST,
                         preferred_element_type=F32) + b_ref[...]


def _ada(c, ada_w, ada_b):
    bsz = c.shape[0]
    n = ada_w.shape[-1]
    tn = D_MODEL
    return pl.pallas_call(
        _ada_kernel,
        grid=(DEPTH, n // tn),
        in_specs=[
            pl.BlockSpec((bsz, D_MODEL), lambda l, j: (0, 0)),
            pl.BlockSpec((None, D_MODEL, tn), lambda l, j: (l, 0, j)),
            pl.BlockSpec((None, 1, tn), lambda l, j: (l, 0, j)),
        ],
        out_specs=pl.BlockSpec((None, bsz, tn), lambda l, j: (l, 0, j)),
        out_shape=jax.ShapeDtypeStruct((DEPTH, bsz, n), F32),
        compiler_params=_params(2),
        name="ada_mod",
    )(c, ada_w, ada_b.reshape(DEPTH, 1, n))


def _ffn_kernel(*refs, mixer_out):
    if mixer_out:
        att_ref, mix_gate_ref, wo_ref = refs[:3]
        refs = refs[3:]
    x_ref, shift_ref, scale_ref, gate_ref, g_ref, w13_ref, w2_ref, o_ref, act_ref = refs
    x = x_ref[...]
    if mixer_out:
        x = x + mix_gate_ref[...] * jnp.dot(att_ref[...], wo_ref[...],
                                            preferred_element_type=F32)
    hb = _modulated(x, g_ref[...], shift_ref[...], scale_ref[...]).astype(BF16)
    for c in range(N_FF_CHUNKS):
        gate = jnp.dot(hb, w13_ref[:, c * FF_CHUNK:(c + 1) * FF_CHUNK],
                       preferred_element_type=F32)
        up = jnp.dot(hb, w13_ref[:, D_FF + c * FF_CHUNK:D_FF + (c + 1) * FF_CHUNK],
                     preferred_element_type=F32)
        act_ref[:, c * FF_CHUNK:(c + 1) * FF_CHUNK] = (
            gate * jax.nn.sigmoid(gate) * up).astype(BF16)
    y = jnp.dot(act_ref[...], w2_ref[...], preferred_element_type=F32)
    o_ref[...] = x + (0.5 * gate_ref[...]) * y


def _ffn(x, shift, scale, gate, g, w13, w2, mixer_out=None):
    bsz, seq, d = x.shape
    tm = ROW_TILE
    row = pl.BlockSpec((None, tm, d), lambda b, i: (b, i, 0))
    vec = pl.BlockSpec((None, 1, d), lambda b, i: (b, 0, 0))
    in_specs = [
        row, vec, vec, vec,
        pl.BlockSpec((1, d), lambda b, i: (0, 0)),
        _resident((d, 2 * D_FF), lambda b, i: (0, 0)),
        _resident((D_FF, d), lambda b, i: (0, 0)),
    ]
    args = (x, shift, scale, gate, g, w13, w2)
    if mixer_out is not None:
        in_specs = [row, vec, _resident((d, d), lambda b, i: (0, 0))] + in_specs
        args = tuple(mixer_out) + args
    return pl.pallas_call(
        functools.partial(_ffn_kernel, mixer_out=mixer_out is not None),
        grid=(bsz, seq // tm),
        in_specs=in_specs,
        out_specs=row,
        out_shape=jax.ShapeDtypeStruct(x.shape, F32),
        scratch_shapes=[pltpu.VMEM((tm, D_FF), BF16)],
        compiler_params=_params(2),
        name="ffn_half_step",
    )(*args)


def _head_norm(t, gain):
    lo = lax.broadcasted_iota(jnp.int32, t.shape, 1) < HEAD_DIM
    t2 = t * t
    ss_lo = jnp.sum(jnp.where(lo, t2, 0.0), axis=-1, keepdims=True)
    ss_hi = jnp.sum(jnp.where(lo, 0.0, t2), axis=-1, keepdims=True)
    rs = jnp.where(lo, lax.rsqrt(ss_lo * (1.0 / HEAD_DIM) + EPS),
                   lax.rsqrt(ss_hi * (1.0 / HEAD_DIM) + EPS))
    return t * rs * gain


def _proj_chunks(hb, w_ref, col0, n_chunks):
    for c in range(0, n_chunks, 2):
        width = min(2, n_chunks - c) * LANES
        t = jnp.dot(hb, w_ref[:, col0 + c * LANES:col0 + c * LANES + width],
                    preferred_element_type=F32)
        for i in range(width // LANES):
            yield c + i, t[:, i * LANES:(i + 1) * LANES]


def _swa_proj_kernel(x_ref, shift_ref, scale_ref, g_ref, w_ref, qg_ref, kg_ref,
                     q_ref, k_ref, v_ref):
    hb = _modulated(x_ref[...], g_ref[...], shift_ref[...], scale_ref[...]).astype(BF16)
    lo = lax.broadcasted_iota(jnp.int32, (hb.shape[0], LANES), 1) < HEAD_DIM
    for c, t in _proj_chunks(hb, w_ref, 0, D_MODEL // LANES):
        q_ref[:, c * LANES:(c + 1) * LANES] = _head_norm(t, qg_ref[...]).astype(BF16)
    kv_cols = A_KV_HEADS * HEAD_DIM

    def store_duplicated(ref, c, t):
        r = pltpu.roll(t, HEAD_DIM, axis=1)
        ref[:, (2 * c) * LANES:(2 * c + 1) * LANES] = jnp.where(lo, t, r).astype(BF16)
        ref[:, (2 * c + 1) * LANES:(2 * c + 2) * LANES] = jnp.where(lo, r, t).astype(BF16)

    for c, t in _proj_chunks(hb, w_ref, D_MODEL, kv_cols // LANES):
        store_duplicated(k_ref, c, _head_norm(t, kg_ref[...]))
    for c, t in _proj_chunks(hb, w_ref, D_MODEL + kv_cols, kv_cols // LANES):
        v_ref[:, c * LANES:(c + 1) * LANES] = t.astype(BF16)


def _swa_proj(x, shift, scale, g, w_in, qg, kg):
    bsz, seq, d = x.shape
    tm = ROW_TILE
    n_in = w_in.shape[1]
    kvw = A_KV_HEADS * LANES
    kv_cols = A_KV_HEADS * HEAD_DIM
    row = lambda n: pl.BlockSpec((None, tm, n), lambda b, i: (b, i, 0))
    vec = pl.BlockSpec((None, 1, d), lambda b, i: (b, 0, 0))
    const = lambda n: pl.BlockSpec((1, n), lambda b, i: (0, 0))
    return pl.pallas_call(
        _swa_proj_kernel,
        grid=(bsz, seq // tm),
        in_specs=[row(d), vec, vec, const(d),
                  _resident((d, n_in), lambda b, i: (0, 0)),
                  const(LANES), const(LANES)],
        out_specs=[row(d), row(kvw), row(kv_cols)],
        out_shape=[jax.ShapeDtypeStruct((bsz, seq, d), BF16),
                   jax.ShapeDtypeStruct((bsz, seq, kvw), BF16),
                   jax.ShapeDtypeStruct((bsz, seq, kv_cols), BF16)],
        compiler_params=_params(2),
        name="swa_proj",
    )(x, shift, scale, g, w_in, qg, kg)


def _swa_attn_kernel(sink_ref, q_ref, k_ref, v_ref, bias_ref, x_ref, gate_ref, wo_ref,
                     o_ref, att_ref):
    tq = q_ref.shape[0]
    q0 = pl.program_id(1) * tq
    zeros = jnp.zeros((HEAD_DIM, BLOCK), BF16)
    ones = jnp.ones((ONES_ROWS, 2 * BLOCK), BF16)
    units = [(sb, j) for sb in range(tq // BLOCK) for j in range(A_KV_HEADS)]

    def window(sb):
        row0 = q0 + sb * BLOCK
        return pl.multiple_of(jnp.maximum(row0 - BLOCK, 0), BLOCK), (row0 == 0).astype(jnp.int32)

    def score_stage(sb, j):
        kstart, first = window(sb)
        rows = slice(sb * BLOCK, (sb + 1) * BLOCK)
        kc = k_ref[pl.ds(kstart, 2 * BLOCK), j * LANES:(j + 1) * LANES]
        wq = []
        for half in range(2):
            qt = q_ref[rows, (2 * j + half) * LANES:(2 * j + half + 1) * LANES]
            qt = qt.astype(F32).T.astype(BF16)
            wq.append(jnp.concatenate([qt[:HEAD_DIM], zeros], axis=0))
            wq.append(jnp.concatenate([zeros, qt[HEAD_DIM:]], axis=0))
        s = jnp.dot(kc, jnp.concatenate(wq, axis=1), preferred_element_type=F32)
        return s + bias_ref[first, j]

    def softmax_stage(j, s):
        sink = sink_ref[j]
        m = jnp.maximum(jnp.max(s, axis=0, keepdims=True), sink)
        return jnp.exp2(s - m).astype(BF16), jnp.exp2(sink - m)

    def pv_stage(sb, j, p, p_sink):
        kstart, _ = window(sb)
        rows = slice(sb * BLOCK, (sb + 1) * BLOCK)
        vt = v_ref[pl.ds(kstart, 2 * BLOCK), (j // 2) * LANES:(j // 2 + 1) * LANES]
        vt = vt.astype(F32).T.astype(BF16)
        v_aug = jnp.concatenate(
            [vt[(j % 2) * HEAD_DIM:(j % 2 + 1) * HEAD_DIM], ones], axis=0)
        acc = jnp.dot(v_aug, p, preferred_element_type=F32)
        ot = acc[:HEAD_DIM] / (acc[HEAD_DIM:HEAD_DIM + 1] + p_sink)
        for half in range(2):
            pair = jnp.concatenate(
                [ot[:, (2 * half) * BLOCK:(2 * half + 1) * BLOCK],
                 ot[:, (2 * half + 1) * BLOCK:(2 * half + 2) * BLOCK]], axis=0)
            ch = 2 * j + half
            att_ref[rows, ch * LANES:(ch + 1) * LANES] = pair.T.astype(BF16)

    scores = score_stage(*units[0])
    pending = None
    for u, (sb, j) in enumerate(units):
        nxt = score_stage(*units[u + 1]) if u + 1 < len(units) else None
        probs = softmax_stage(j, scores)
        if pending is not None:
            pv_stage(*pending)
        pending = (sb, j) + probs
        scores = nxt
    pv_stage(*pending)
    y = jnp.dot(att_ref[...], wo_ref[...], preferred_element_type=F32)
    o_ref[...] = x_ref[...] + gate_ref[...] * y


def _swa_attn(sink, q, k2, v, bias, x, gate, w_out):
    bsz, seq, d = x.shape
    tq = ROW_TILE
    kvw = k2.shape[-1]
    row = lambda n: pl.BlockSpec((None, tq, n), lambda b, i: (b, i, 0))
    return pl.pallas_call(
        _swa_attn_kernel,
        grid=(bsz, seq // tq),
        in_specs=[
            pl.BlockSpec(sink.shape, lambda b, i: (0, 0, 0)),
            row(d),
            _resident((None, seq, kvw), lambda b, i: (b, 0, 0)),
            _resident((None, seq, v.shape[-1]), lambda b, i: (b, 0, 0)),
            _resident(bias.shape, lambda b, i: (0, 0, 0, 0)),
            row(d),
            pl.BlockSpec((None, 1, d), lambda b, i: (b, 0, 0)),
            _resident((d, d), lambda b, i: (0, 0)),
        ],
        out_specs=row(d),
        out_shape=jax.ShapeDtypeStruct(x.shape, F32),
        scratch_shapes=[pltpu.VMEM((tq, d), BF16)],
        compiler_params=_params(2),
        name="swa_attn_out",
    )(sink, q, k2, v, bias, x, gate, w_out)


def _rel_bucket_table():
    n = np.arange(WINDOW)
    max_exact = REL_BUCKETS // 2
    nf = np.maximum(n, 1).astype(np.float32)
    large = max_exact + (np.log(nf / max_exact) / math.log(REL_MAX_DIST / max_exact)
                         * (REL_BUCKETS - max_exact)).astype(np.int32)
    large = np.minimum(large, REL_BUCKETS - 1)
    return np.where(n < max_exact, n, large)


def _swa_bias_tiles(rel_bias):
    by_dist = rel_bias.astype(F32)[_rel_bucket_table()].T
    rev = by_dist[:, ::-1]
    n_h = rev.shape[0]
    span = 3 * BLOCK - 1
    tiles = []
    for key_offset in (BLOCK, 0):
        neg = lambda n: jnp.full((n_h, n), NEG, F32)
        w = jnp.concatenate([neg(key_offset), rev, neg(span - key_offset - WINDOW)], axis=1)
        z = jnp.concatenate([w[:, BLOCK - 1:], neg(1), w[:, :BLOCK - 1]], axis=1)
        length = z.shape[1]
        t = jnp.tile(z, (1, BLOCK))[:, :BLOCK * (length - 1)]
        t = t.reshape(n_h, BLOCK, length - 1)[:, :, :2 * BLOCK]
        b = t.reshape(A_KV_HEADS, A_GROUP * BLOCK, 2 * BLOCK)
        tiles.append(jnp.transpose(b, (0, 2, 1)))
    return jnp.stack(tiles)


def _split3(x):
    hi = x.astype(BF16).astype(F32)
    rest = x - hi
    mid = rest.astype(BF16).astype(F32)
    low = (rest - mid).astype(BF16).astype(F32)
    return (hi + pltpu.roll(mid, N_HEADS, axis=1)
            + pltpu.roll(low, 2 * N_HEADS, axis=1)).astype(BF16)


def _fox_proj_kernel(x_ref, shift_ref, scale_ref, g_ref, w_ref, bf_ref, qg_ref, kg_ref,
                     place_ref, q_ref, k_ref, v_ref, carry_ref):
    tm = x_ref.shape[0]
    hb = _modulated(x_ref[...], g_ref[...], shift_ref[...], scale_ref[...]).astype(BF16)
    n_c = D_MODEL // LANES
    for c, t in _proj_chunks(hb, w_ref, 0, n_c):
        q_ref[:, c * LANES:(c + 1) * LANES] = _head_norm(t, qg_ref[...]).astype(BF16)
    for c, t in _proj_chunks(hb, w_ref, 2 * D_MODEL, n_c):
        v_ref[:, c * LANES:(c + 1) * LANES] = t.astype(BF16)

    z = jnp.dot(hb, w_ref[:, 3 * D_MODEL:3 * D_MODEL + LANES],
                preferred_element_type=F32) + bf_ref[...]
    lane = lax.broadcasted_iota(jnp.int32, (tm, LANES), 1)
    live = lane < N_HEADS
    log_f = jnp.where(live, jnp.minimum(z, 0.0) - jnp.log1p(jnp.exp(-jnp.abs(z))), 0.0)
    r = lax.broadcasted_iota(jnp.int32, (tm, tm), 0)
    cidx = lax.broadcasted_iota(jnp.int32, (tm, tm), 1)
    tri = (cidx <= r).astype(BF16)
    part = jnp.dot(tri, _split3(log_f), preferred_element_type=F32)

    @pl.when(pl.program_id(1) == 0)
    def _():
        carry_ref[...] = jnp.zeros_like(carry_ref)

    cum = (part + pltpu.roll(part, LANES - N_HEADS, axis=1)
           + pltpu.roll(part, LANES - 2 * N_HEADS, axis=1)) + carry_ref[...]
    carry_ref[...] = cum[tm - 1:tm, :]

    aug = jnp.dot(_split3(jnp.where(live, cum * LOG2E, 0.0)), place_ref[...],
                  preferred_element_type=F32)
    lo = lane < HEAD_DIM
    for c, t in _proj_chunks(hb, w_ref, D_MODEL, n_c):
        t = _head_norm(t, kg_ref[...])
        even, odd = 2 * c, 2 * c + 1
        k_ref[:, even * LANES:(even + 1) * LANES] = jnp.where(
            lo, t, aug[:, even * LANES:(even + 1) * LANES]).astype(BF16)
        k_ref[:, odd * LANES:(odd + 1) * LANES] = jnp.where(
            lo, aug[:, odd * LANES:(odd + 1) * LANES], t).astype(BF16)


def _fox_proj(x, shift, scale, g, w_in, b_f, qg, kg):
    bsz, seq, d = x.shape
    tm = ROW_TILE
    n_in = w_in.shape[1]
    row = pl.BlockSpec((None, tm, d), lambda b, i: (b, i, 0))
    vec = pl.BlockSpec((None, 1, d), lambda b, i: (b, 0, 0))
    const = lambda n: pl.BlockSpec((1, n), lambda b, i: (0, 0))
    act = jax.ShapeDtypeStruct((bsz, seq, d), BF16)
    kw = N_HEADS * LANES
    place = np.zeros((LANES, kw), np.float32)
    for h in range(N_HEADS):
        for term in range(FORGET_TERMS):
            first = HEAD_DIM if h % 2 == 0 else 0
            place[term * N_HEADS + h, h * LANES + first + term] = 1.0
    return pl.pallas_call(
        _fox_proj_kernel,
        grid=(bsz, seq // tm),
        in_specs=[row, vec, vec, const(d),
                  _resident((d, n_in), lambda b, i: (0, 0)),
                  const(LANES), const(LANES), const(LANES),
                  _resident((LANES, kw), lambda b, i: (0, 0))],
        out_specs=[row, pl.BlockSpec((None, tm, kw), lambda b, i: (b, i, 0)), row],
        out_shape=[act, jax.ShapeDtypeStruct((bsz, seq, kw), BF16), act],
        scratch_shapes=[pltpu.VMEM((1, LANES), F32)],
        compiler_params=_params(2),
        name="fox_proj",
    )(x, shift, scale, g, w_in, b_f, qg, kg, jnp.asarray(place, BF16))


def _fox_attn_kernel(q_ref, k_ref, v_ref, o_ref, vt_ref, wq_ref,
                     s00_ref, s01_ref, s10_ref, s11_ref, p00_ref, p01_ref, p10_ref, p11_ref,
                     cmax_ref, m_ref, alpha_ref, acc_ref):
    s_refs = ((s00_ref, s01_ref), (s10_ref, s11_ref))
    p_refs = ((p00_ref, p01_ref), (p10_ref, p11_ref))
    tq = q_ref.shape[0]
    tk = ATT_K_TILE
    seq = k_ref.shape[0]
    qi = pl.program_id(2)

    @pl.when(qi == 0)
    def _():
        for c in range(seq // tk):
            blk = v_ref[c * tk:(c + 1) * tk, :].astype(F32)
            vt_ref[:, c * tk:(c + 1) * tk] = blk.T.astype(BF16)

    q0 = pl.multiple_of(qi * tq, tq)
    qt = q_ref[...].astype(F32).T.astype(BF16)
    row = lax.broadcasted_iota(jnp.int32, (HEAD_DIM, tq), 0)
    minus_ones = jnp.where(row < FORGET_TERMS, -1.0, 0.0).astype(BF16)
    wq_ref[0] = jnp.concatenate([qt[:HEAD_DIM], minus_ones], axis=0)
    wq_ref[1] = jnp.concatenate([minus_ones, qt[HEAD_DIM:]], axis=0)
    ones = jnp.ones((ONES_ROWS, tk), BF16)
    m_ref[...] = jnp.full_like(m_ref, NEG)
    alpha_ref[...] = jnp.zeros_like(alpha_ref)
    acc_ref[...] = jnp.zeros_like(acc_ref)
    for a in range(2):
        p_refs[1][a][...] = jnp.zeros_like(p_refs[1][a])

    n_ct = tq // ATT_COL_TILE

    def cols(ct):
        return slice(ct * ATT_COL_TILE, (ct + 1) * ATT_COL_TILE)

    def score_stage(j, slot, ct):
        k0 = pl.multiple_of(j * tk, tk)
        for a in range(2):
            s = jnp.dot(k_ref[pl.ds(k0, tk), a * LANES:(a + 1) * LANES],
                        wq_ref[a, :, cols(ct)], preferred_element_type=F32)
            s_refs[slot][a][:, cols(ct)] = s
            cmax_ref[slot, a, :, cols(ct)] = jnp.max(s, axis=0, keepdims=True)

    def softmax_stage(j, slot, ct, masked):
        k0 = pl.multiple_of(j * tk, tk)
        for a in range(2):
            if masked:
                s = s_refs[slot][a][:, cols(ct)]
                key = k0 + lax.broadcasted_iota(jnp.int32, s.shape, 0)
                qry = (q0 + ct * ATT_COL_TILE) + lax.broadcasted_iota(jnp.int32, s.shape, 1)
                s = jnp.where(key <= qry, s, NEG)
                s_refs[slot][a][:, cols(ct)] = s
                col_max = jnp.max(s, axis=0, keepdims=True)
            else:
                col_max = cmax_ref[slot, a, :, cols(ct)]
            m_old = m_ref[a, :, cols(ct)]
            m_new = jnp.maximum(m_old, col_max)
            alpha_ref[a, :, cols(ct)] = jnp.exp2(m_old - m_new)
            p_refs[slot][a][:, cols(ct)] = jnp.exp2(
                s_refs[slot][a][:, cols(ct)] - m_new).astype(BF16)
            m_ref[a, :, cols(ct)] = m_new

    def pv_stage(j, slot, ct):
        k0 = pl.multiple_of(jnp.maximum(j, 0) * tk, tk)
        for a in range(2):
            v_aug = jnp.concatenate(
                [vt_ref[a * HEAD_DIM:(a + 1) * HEAD_DIM, pl.ds(k0, tk)], ones], axis=0)
            acc_ref[a, :, cols(ct)] = (
                alpha_ref[a, :, cols(ct)] * acc_ref[a, :, cols(ct)]
                + jnp.dot(v_aug, p_refs[slot][a][:, cols(ct)], preferred_element_type=F32))

    n_diag = tq // tk
    n_full = qi * n_diag

    def visibility(d, ct):
        if d < 0 or d >= n_diag:
            return "full" if d < 0 else "skip"
        first_key, last_key = d * tk, (d + 1) * tk - 1
        first_qry, last_qry = ct * ATT_COL_TILE, (ct + 1) * ATT_COL_TILE - 1
        if last_key <= first_qry:
            return "full"
        return "skip" if first_key > last_qry else "mask"

    def block_step(j, slot, d=-1):
        nxt = -1 if d < 0 else d + 1
        for ct in range(n_ct):
            if visibility(d - 1, ct) != "skip":
                pv_stage(j - 1, 1 - slot, ct)
            if visibility(nxt, ct) != "skip":
                score_stage(j + 1, 1 - slot, ct)
            if visibility(d, ct) != "skip":
                softmax_stage(j, slot, ct, visibility(d, ct) == "mask")

    for ct in range(n_ct):
        score_stage(0, 0, ct)

    def pair(ii, carry):
        block_step(2 * ii, 0)
        block_step(2 * ii + 1, 1)
        return carry

    lax.fori_loop(0, n_full // 2, pair, 0)
    for d in range(n_diag):
        block_step(n_full + d, d % 2, d)
    for ct in range(n_ct):
        if visibility(n_diag - 1, ct) != "skip":
            pv_stage(n_full + n_diag - 1, (n_diag - 1) % 2, ct)

    outs = []
    for a in range(2):
        acc = acc_ref[a]
        outs.append((acc[:HEAD_DIM] / acc[HEAD_DIM:HEAD_DIM + 1]).T)
    o_ref[...] = jnp.concatenate(outs, axis=1).astype(BF16)


def _fox_attn(q, k, v):
    bsz, seq, d = q.shape
    tq, tk = ATT_Q_TILE, ATT_K_TILE
    n_pairs = d // LANES
    return pl.pallas_call(
        _fox_attn_kernel,
        grid=(bsz, n_pairs, seq // tq),
        in_specs=[
            pl.BlockSpec((None, tq, LANES), lambda b, h, i: (b, i, h)),
            pl.BlockSpec((None, seq, 2 * LANES), lambda b, h, i: (b, 0, h)),
            pl.BlockSpec((None, seq, LANES), lambda b, h, i: (b, 0, h)),
        ],
        out_specs=pl.BlockSpec((None, tq, LANES), lambda b, h, i: (b, i, h)),
        out_shape=jax.ShapeDtypeStruct((bsz, seq, d), BF16),
        scratch_shapes=(
            [pltpu.VMEM((LANES, seq), BF16), pltpu.VMEM((2, LANES, tq), BF16)]
            + [pltpu.VMEM((tk, tq), F32)] * 4 + [pltpu.VMEM((tk, tq), BF16)] * 4
            + [pltpu.VMEM((2, 2, 1, tq), F32),
               pltpu.VMEM((2, 1, tq), F32), pltpu.VMEM((2, 1, tq), F32),
               pltpu.VMEM((2, HEAD_DIM + ONES_ROWS, tq), F32)]),
        compiler_params=_params(3),
        name="fox_attn",
    )(q, k, v)


def _head_gain(gain, scale):
    return (jnp.tile(gain.astype(F32), LANES // HEAD_DIM) * scale).reshape(1, LANES)


def kernel(x, c, ada_w, ada_b, norm_g, ffn_w13, ffn_w2, rel_bias, swa_w_in, swa_w_out,
           swa_q_g, swa_k_g, swa_sink, fox_w_in, fox_w_out, fox_b_f, fox_q_g, fox_k_g):
    bsz = x.shape[0]
    mod = _ada(c, ada_w, ada_b).reshape(DEPTH, bsz, 3, 3, 1, D_MODEL)
    q_scale = HEAD_DIM ** -0.5
    bias_tiles = _swa_bias_tiles(rel_bias.astype(F32) * LOG2E)

    w13_b = ffn_w13.astype(BF16)
    w2_b = ffn_w2.astype(BF16)

    for layer in range(DEPTH):
        g = norm_g[layer].reshape(3, 1, D_MODEL)
        m = mod[layer]
        x = _ffn(x, m[:, 0, 0], m[:, 0, 1], m[:, 0, 2], g[0], w13_b[layer, 0], w2_b[layer, 0])
        j = layer // 2
        mixer_out = None
        if layer % 2 == 0:
            q, k2, v2 = _swa_proj(x, m[:, 1, 0], m[:, 1, 1], g[1], swa_w_in[j].astype(BF16),
                                  _head_gain(swa_q_g[j], q_scale * LOG2E),
                                  _head_gain(swa_k_g[j], 1.0))
            sink = jnp.repeat((swa_sink[j].astype(F32) * LOG2E).reshape(A_KV_HEADS, A_GROUP), BLOCK,
                              axis=1).reshape(A_KV_HEADS, 1, A_GROUP * BLOCK)
            x = _swa_attn(sink, q, k2, v2, bias_tiles, x, m[:, 1, 2],
                          swa_w_out[j].astype(BF16))
        else:
            w_in = jnp.pad(fox_w_in[j], ((0, 0), (0, LANES - N_HEADS))).astype(BF16)
            b_f = jnp.pad(fox_b_f[j].astype(F32), (0, LANES - N_HEADS)).reshape(1, LANES)
            q, k, v = _fox_proj(x, m[:, 1, 0], m[:, 1, 1], g[1], w_in, b_f,
                                _head_gain(fox_q_g[j], q_scale * LOG2E),
                                _head_gain(fox_k_g[j], 1.0))
            mixer_out = (_fox_attn(q, k, v), m[:, 1, 2], fox_w_out[j].astype(BF16))
        x = _ffn(x, m[:, 2, 0], m[:, 2, 1], m[:, 2, 2], g[2], w13_b[layer, 1], w2_b[layer, 1],
                 mixer_out=mixer_out)
    return x
```

```python
import functools
import math

import numpy as np
import jax
import jax.numpy as jnp
from jax import lax
from jax.experimental import pallas as pl
from jax.experimental.pallas import tpu as pltpu

D_MODEL = 1024
DEPTH = 4
HEAD_DIM = 64
N_HEADS = D_MODEL // HEAD_DIM
A_KV_HEADS = 4
A_GROUP = N_HEADS // A_KV_HEADS
WINDOW = 128
BLOCK = 128
REL_BUCKETS = 32
REL_MAX_DIST = 128
D_FF = 2816
EPS = 1e-6

LANES = 128
FF_CHUNK = 256
N_FF_CHUNKS = D_FF // FF_CHUNK
ROW_TILE = 512
ATT_Q_TILE = 2048
ATT_K_TILE = 512
ATT_COL_TILE = 256
ONES_ROWS = 16
FORGET_TERMS = 3
LOG2E = 1.4426950408889634
NEG = -1e30
VMEM_LIMIT = 56 * 1024 * 1024

BF16 = jnp.bfloat16
F32 = jnp.float32


def _params(n_axes):
    return pltpu.CompilerParams(
        dimension_semantics=("arbitrary",) * n_axes, vmem_limit_bytes=VMEM_LIMIT)


def _resident(shape, index_map):
    return pl.BlockSpec(shape, index_map, pipeline_mode=pl.Buffered(1))


def _modulated(x, g, shift, scale):
    ms = jnp.mean(x * x, axis=-1, keepdims=True)
    return (x * lax.rsqrt(ms + EPS) * g) * (1.0 + scale) + shift


def _ada_kernel(c_ref, w_ref, b_ref, o_ref):
    c = c_ref[...]
    c_act = c * jax.nn.sigmoid(c)
    o_ref[...] = jnp.dot(c_act, w_ref[...], precision=lax.Precision.HIGHEST,
                         preferred_element_type=F32) + b_ref[...]


def _ada(c, ada_w, ada_b):
    bsz = c.shape[0]
    n = ada_w.shape[-1]
    tn = D_MODEL
    return pl.pallas_call(
        _ada_kernel,
        grid=(DEPTH, n // tn),
        in_specs=[
            pl.BlockSpec((bsz, D_MODEL), lambda l, j: (0, 0)),
            pl.BlockSpec((None, D_MODEL, tn), lambda l, j: (l, 0, j)),
            pl.BlockSpec((None, 1, tn), lambda l, j: (l, 0, j)),
        ],
        out_specs=pl.BlockSpec((None, bsz, tn), lambda l, j: (l, 0, j)),
        out_shape=jax.ShapeDtypeStruct((DEPTH, bsz, n), F32),
        compiler_params=_params(2),
        name="ada_mod",
    )(c, ada_w, ada_b.reshape(DEPTH, 1, n))


def _ffn_kernel(*refs, mixer_out):
    if mixer_out:
        att_ref, mix_gate_ref, wo_ref = refs[:3]
        refs = refs[3:]
    x_ref, shift_ref, scale_ref, gate_ref, g_ref, w13_ref, w2_ref, o_ref, act_ref = refs
    x = x_ref[...]
    if mixer_out:
        x = x + mix_gate_ref[...] * jnp.dot(att_ref[...], wo_ref[...],
                                            preferred_element_type=F32)
    hb = _modulated(x, g_ref[...], shift_ref[...], scale_ref[...]).astype(BF16)
    for c in range(N_FF_CHUNKS):
        gate = jnp.dot(hb, w13_ref[:, c * FF_CHUNK:(c + 1) * FF_CHUNK],
                       preferred_element_type=F32)
        up = jnp.dot(hb, w13_ref[:, D_FF + c * FF_CHUNK:D_FF + (c + 1) * FF_CHUNK],
                     preferred_element_type=F32)
        act_ref[:, c * FF_CHUNK:(c + 1) * FF_CHUNK] = (
            gate * jax.nn.sigmoid(gate) * up).astype(BF16)
    y = jnp.dot(act_ref[...], w2_ref[...], preferred_element_type=F32)
    o_ref[...] = x + (0.5 * gate_ref[...]) * y


def _ffn(x, shift, scale, gate, g, w13, w2, mixer_out=None):
    bsz, seq, d = x.shape
    tm = ROW_TILE
    row = pl.BlockSpec((None, tm, d), lambda b, i: (b, i, 0))
    vec = pl.BlockSpec((None, 1, d), lambda b, i: (b, 0, 0))
    in_specs = [
        row, vec, vec, vec,
        pl.BlockSpec((1, d), lambda b, i: (0, 0)),
        _resident((d, 2 * D_FF), lambda b, i: (0, 0)),
        _resident((D_FF, d), lambda b, i: (0, 0)),
    ]
    args = (x, shift, scale, gate, g, w13, w2)
    if mixer_out is not None:
        in_specs = [row, vec, _resident((d, d), lambda b, i: (0, 0))] + in_specs
        args = tuple(mixer_out) + args
    return pl.pallas_call(
        functools.partial(_ffn_kernel, mixer_out=mixer_out is not None),
        grid=(bsz, seq // tm),
        in_specs=in_specs,
        out_specs=row,
        out_shape=jax.ShapeDtypeStruct(x.shape, F32),
        scratch_shapes=[pltpu.VMEM((tm, D_FF), BF16)],
        compiler_params=_params(2),
        name="ffn_half_step",
    )(*args)


def _head_norm(t, gain):
    lo = lax.broadcasted_iota(jnp.int32, t.shape, 1) < HEAD_DIM
    t2 = t * t
    ss_lo = jnp.sum(jnp.where(lo, t2, 0.0), axis=-1, keepdims=True)
    ss_hi = jnp.sum(jnp.where(lo, 0.0, t2), axis=-1, keepdims=True)
    rs = jnp.where(lo, lax.rsqrt(ss_lo * (1.0 / HEAD_DIM) + EPS),
                   lax.rsqrt(ss_hi * (1.0 / HEAD_DIM) + EPS))
    return t * rs * gain


def _proj_chunks(hb, w_ref, col0, n_chunks):
    for c in range(0, n_chunks, 2):
        width = min(2, n_chunks - c) * LANES
        t = jnp.dot(hb, w_ref[:, col0 + c * LANES:col0 + c * LANES + width],
                    preferred_element_type=F32)
        for i in range(width // LANES):
            yield c + i, t[:, i * LANES:(i + 1) * LANES]


def _swa_proj_kernel(x_ref, shift_ref, scale_ref, g_ref, w_ref, qg_ref, kg_ref,
                     q_ref, k_ref, v_ref):
    hb = _modulated(x_ref[...], g_ref[...], shift_ref[...], scale_ref[...]).astype(BF16)
    lo = lax.broadcasted_iota(jnp.int32, (hb.shape[0], LANES), 1) < HEAD_DIM
    for c, t in _proj_chunks(hb, w_ref, 0, D_MODEL // LANES):
        q_ref[:, c * LANES:(c + 1) * LANES] = _head_norm(t, qg_ref[...]).astype(BF16)
    kv_cols = A_KV_HEADS * HEAD_DIM

    def store_duplicated(ref, c, t):
        r = pltpu.roll(t, HEAD_DIM, axis=1)
        ref[:, (2 * c) * LANES:(2 * c + 1) * LANES] = jnp.where(lo, t, r).astype(BF16)
        ref[:, (2 * c + 1) * LANES:(2 * c + 2) * LANES] = jnp.where(lo, r, t).astype(BF16)

    for c, t in _proj_chunks(hb, w_ref, D_MODEL, kv_cols // LANES):
        store_duplicated(k_ref, c, _head_norm(t, kg_ref[...]))
    for c, t in _proj_chunks(hb, w_ref, D_MODEL + kv_cols, kv_cols // LANES):
        v_ref[:, c * LANES:(c + 1) * LANES] = t.astype(BF16)


def _swa_proj(x, shift, scale, g, w_in, qg, kg):
    bsz, seq, d = x.shape
    tm = ROW_TILE
    n_in = w_in.shape[1]
    kvw = A_KV_HEADS * LANES
    kv_cols = A_KV_HEADS * HEAD_DIM
    row = lambda n: pl.BlockSpec((None, tm, n), lambda b, i: (b, i, 0))
    vec = pl.BlockSpec((None, 1, d), lambda b, i: (b, 0, 0))
    const = lambda n: pl.BlockSpec((1, n), lambda b, i: (0, 0))
    return pl.pallas_call(
        _swa_proj_kernel,
        grid=(bsz, seq // tm),
        in_specs=[row(d), vec, vec, const(d),
                  _resident((d, n_in), lambda b, i: (0, 0)),
                  const(LANES), const(LANES)],
        out_specs=[row(d), row(kvw), row(kv_cols)],
        out_shape=[jax.ShapeDtypeStruct((bsz, seq, d), BF16),
                   jax.ShapeDtypeStruct((bsz, seq, kvw), BF16),
                   jax.ShapeDtypeStruct((bsz, seq, kv_cols), BF16)],
        compiler_params=_params(2),
        name="swa_proj",
    )(x, shift, scale, g, w_in, qg, kg)


def _swa_attn_kernel(sink_ref, q_ref, k_ref, v_ref, bias_ref, x_ref, gate_ref, wo_ref,
                     o_ref, att_ref):
    tq = q_ref.shape[0]
    q0 = pl.program_id(1) * tq
    zeros = jnp.zeros((HEAD_DIM, BLOCK), BF16)
    ones = jnp.ones((ONES_ROWS, 2 * BLOCK), BF16)
    units = [(sb, j) for sb in range(tq // BLOCK) for j in range(A_KV_HEADS)]

    def window(sb):
        row0 = q0 + sb * BLOCK
        return pl.multiple_of(jnp.maximum(row0 - BLOCK, 0), BLOCK), (row0 == 0).astype(jnp.int32)

    def score_stage(sb, j):
        kstart, first = window(sb)
        rows = slice(sb * BLOCK, (sb + 1) * BLOCK)
        kc = k_ref[pl.ds(kstart, 2 * BLOCK), j * LANES:(j + 1) * LANES]
        wq = []
        for half in range(2):
            qt = q_ref[rows, (2 * j + half) * LANES:(2 * j + half + 1) * LANES]
            qt = qt.astype(F32).T.astype(BF16)
            wq.append(jnp.concatenate([qt[:HEAD_DIM], zeros], axis=0))
            wq.append(jnp.concatenate([zeros, qt[HEAD_DIM:]], axis=0))
        s = jnp.dot(kc, jnp.concatenate(wq, axis=1), preferred_element_type=F32)
        return s + bias_ref[first, j]

    def softmax_stage(j, s):
        sink = sink_ref[j]
        m = jnp.maximum(jnp.max(s, axis=0, keepdims=True), sink)
        return jnp.exp2(s - m).astype(BF16), jnp.exp2(sink - m)

    def pv_stage(sb, j, p, p_sink):
        kstart, _ = window(sb)
        rows = slice(sb * BLOCK, (sb + 1) * BLOCK)
        vt = v_ref[pl.ds(kstart, 2 * BLOCK), (j // 2) * LANES:(j // 2 + 1) * LANES]
        vt = vt.astype(F32).T.astype(BF16)
        v_aug = jnp.concatenate(
            [vt[(j % 2) * HEAD_DIM:(j % 2 + 1) * HEAD_DIM], ones], axis=0)
        acc = jnp.dot(v_aug, p, preferred_element_type=F32)
        ot = acc[:HEAD_DIM] / (acc[HEAD_DIM:HEAD_DIM + 1] + p_sink)
        for half in range(2):
            pair = jnp.concatenate(
                [ot[:, (2 * half) * BLOCK:(2 * half + 1) * BLOCK],
                 ot[:, (2 * half + 1) * BLOCK:(2 * half + 2) * BLOCK]], axis=0)
            ch = 2 * j + half
            att_ref[rows, ch * LANES:(ch + 1) * LANES] = pair.T.astype(BF16)

    scores = score_stage(*units[0])
    pending = None
    for u, (sb, j) in enumerate(units):
        nxt = score_stage(*units[u + 1]) if u + 1 < len(units) else None
        probs = softmax_stage(j, scores)
        if pending is not None:
            pv_stage(*pending)
        pending = (sb, j) + probs
        scores = nxt
    pv_stage(*pending)
    y = jnp.dot(att_ref[...], wo_ref[...], preferred_element_type=F32)
    o_ref[...] = x_ref[...] + gate_ref[...] * y


def _swa_attn(sink, q, k2, v, bias, x, gate, w_out):
    bsz, seq, d = x.shape
    tq = ROW_TILE
    kvw = k2.shape[-1]
    row = lambda n: pl.BlockSpec((None, tq, n), lambda b, i: (b, i, 0))
    return pl.pallas_call(
        _swa_attn_kernel,
        grid=(bsz, seq // tq),
        in_specs=[
            pl.BlockSpec(sink.shape, lambda b, i: (0, 0, 0)),
            row(d),
            _resident((None, seq, kvw), lambda b, i: (b, 0, 0)),
            _resident((None, seq, v.shape[-1]), lambda b, i: (b, 0, 0)),
            _resident(bias.shape, lambda b, i: (0, 0, 0, 0)),
            row(d),
            pl.BlockSpec((None, 1, d), lambda b, i: (b, 0, 0)),
            _resident((d, d), lambda b, i: (0, 0)),
        ],
        out_specs=row(d),
        out_shape=jax.ShapeDtypeStruct(x.shape, F32),
        scratch_shapes=[pltpu.VMEM((tq, d), BF16)],
        compiler_params=_params(2),
        name="swa_attn_out",
    )(sink, q, k2, v, bias, x, gate, w_out)


def _rel_bucket_table():
    n = np.arange(WINDOW)
    max_exact = REL_BUCKETS // 2
    nf = np.maximum(n, 1).astype(np.float32)
    large = max_exact + (np.log(nf / max_exact) / math.log(REL_MAX_DIST / max_exact)
                         * (REL_BUCKETS - max_exact)).astype(np.int32)
    large = np.minimum(large, REL_BUCKETS - 1)
    return np.where(n < max_exact, n, large)


def _swa_bias_tiles(rel_bias):
    by_dist = rel_bias.astype(F32)[_rel_bucket_table()].T
    rev = by_dist[:, ::-1]
    n_h = rev.shape[0]
    span = 3 * BLOCK - 1
    tiles = []
    for key_offset in (BLOCK, 0):
        neg = lambda n: jnp.full((n_h, n), NEG, F32)
        w = jnp.concatenate([neg(key_offset), rev, neg(span - key_offset - WINDOW)], axis=1)
        z = jnp.concatenate([w[:, BLOCK - 1:], neg(1), w[:, :BLOCK - 1]], axis=1)
        length = z.shape[1]
        t = jnp.tile(z, (1, BLOCK))[:, :BLOCK * (length - 1)]
        t = t.reshape(n_h, BLOCK, length - 1)[:, :, :2 * BLOCK]
        b = t.reshape(A_KV_HEADS, A_GROUP * BLOCK, 2 * BLOCK)
        tiles.append(jnp.transpose(b, (0, 2, 1)))
    return jnp.stack(tiles)


def _split3(x):
    hi = x.astype(BF16).astype(F32)
    rest = x - hi
    mid = rest.astype(BF16).astype(F32)
    low = (rest - mid).astype(BF16).astype(F32)
    return (hi + pltpu.roll(mid, N_HEADS, axis=1)
            + pltpu.roll(low, 2 * N_HEADS, axis=1)).astype(BF16)


def _fox_proj_kernel(x_ref, shift_ref, scale_ref, g_ref, w_ref, bf_ref, qg_ref, kg_ref,
                     place_ref, q_ref, k_ref, v_ref, carry_ref):
    tm = x_ref.shape[0]
    hb = _modulated(x_ref[...], g_ref[...], shift_ref[...], scale_ref[...]).astype(BF16)
    n_c = D_MODEL // LANES
    for c, t in _proj_chunks(hb, w_ref, 0, n_c):
        q_ref[:, c * LANES:(c + 1) * LANES] = _head_norm(t, qg_ref[...]).astype(BF16)
    for c, t in _proj_chunks(hb, w_ref, 2 * D_MODEL, n_c):
        v_ref[:, c * LANES:(c + 1) * LANES] = t.astype(BF16)

    z = jnp.dot(hb, w_ref[:, 3 * D_MODEL:3 * D_MODEL + LANES],
                preferred_element_type=F32) + bf_ref[...]
    lane = lax.broadcasted_iota(jnp.int32, (tm, LANES), 1)
    live = lane < N_HEADS
    log_f = jnp.where(live, jnp.minimum(z, 0.0) - jnp.log1p(jnp.exp(-jnp.abs(z))), 0.0)
    r = lax.broadcasted_iota(jnp.int32, (tm, tm), 0)
    cidx = lax.broadcasted_iota(jnp.int32, (tm, tm), 1)
    tri = (cidx <= r).astype(BF16)
    part = jnp.dot(tri, _split3(log_f), preferred_element_type=F32)

    @pl.when(pl.program_id(1) == 0)
    def _():
        carry_ref[...] = jnp.zeros_like(carry_ref)

    cum = (part + pltpu.roll(part, LANES - N_HEADS, axis=1)
           + pltpu.roll(part, LANES - 2 * N_HEADS, axis=1)) + carry_ref[...]
    carry_ref[...] = cum[tm - 1:tm, :]

    aug = jnp.dot(_split3(jnp.where(live, cum * LOG2E, 0.0)), place_ref[...],
                  preferred_element_type=F32)
    lo = lane < HEAD_DIM
    for c, t in _proj_chunks(hb, w_ref, D_MODEL, n_c):
        t = _head_norm(t, kg_ref[...])
        even, odd = 2 * c, 2 * c + 1
        k_ref[:, even * LANES:(even + 1) * LANES] = jnp.where(
            lo, t, aug[:, even * LANES:(even + 1) * LANES]).astype(BF16)
        k_ref[:, odd * LANES:(odd + 1) * LANES] = jnp.where(
            lo, aug[:, odd * LANES:(odd + 1) * LANES], t).astype(BF16)


def _fox_proj(x, shift, scale, g, w_in, b_f, qg, kg):
    bsz, seq, d = x.shape
    tm = ROW_TILE
    n_in = w_in.shape[1]
    row = pl.BlockSpec((None, tm, d), lambda b, i: (b, i, 0))
    vec = pl.BlockSpec((None, 1, d), lambda b, i: (b, 0, 0))
    const = lambda n: pl.BlockSpec((1, n), lambda b, i: (0, 0))
    act = jax.ShapeDtypeStruct((bsz, seq, d), BF16)
    kw = N_HEADS * LANES
    place = np.zeros((LANES, kw), np.float32)
    for h in range(N_HEADS):
        for term in range(FORGET_TERMS):
            first = HEAD_DIM if h % 2 == 0 else 0
            place[term * N_HEADS + h, h * LANES + first + term] = 1.0
    return pl.pallas_call(
        _fox_proj_kernel,
        grid=(bsz, seq // tm),
        in_specs=[row, vec, vec, const(d),
                  _resident((d, n_in), lambda b, i: (0, 0)),
                  const(LANES), const(LANES), const(LANES),
                  _resident((LANES, kw), lambda b, i: (0, 0))],
        out_specs=[row, pl.BlockSpec((None, tm, kw), lambda b, i: (b, i, 0)), row],
        out_shape=[act, jax.ShapeDtypeStruct((bsz, seq, kw), BF16), act],
        scratch_shapes=[pltpu.VMEM((1, LANES), F32)],
        compiler_params=_params(2),
        name="fox_proj",
    )(x, shift, scale, g, w_in, b_f, qg, kg, jnp.asarray(place, BF16))


def _fox_attn_kernel(q_ref, k_ref, v_ref, o_ref, vt_ref, wq_ref,
                     s00_ref, s01_ref, s10_ref, s11_ref, cmax_ref, m_ref, acc_ref):
    s_refs = ((s00_ref, s01_ref), (s10_ref, s11_ref))
    tq = q_ref.shape[0]
    tk = ATT_K_TILE
    seq = k_ref.shape[0]
    qi = pl.program_id(2)

    @pl.when(qi == 0)
    def _():
        for c in range(seq // tk):
            blk = v_ref[c * tk:(c + 1) * tk, :].astype(F32)
            vt_ref[:, c * tk:(c + 1) * tk] = blk.T.astype(BF16)

    q0 = pl.multiple_of(qi * tq, tq)
    qt = q_ref[...].astype(F32).T.astype(BF16)
    row = lax.broadcasted_iota(jnp.int32, (HEAD_DIM, tq), 0)
    minus_ones = jnp.where(row < FORGET_TERMS, -1.0, 0.0).astype(BF16)
    wq_ref[0] = jnp.concatenate([qt[:HEAD_DIM], minus_ones], axis=0)
    wq_ref[1] = jnp.concatenate([minus_ones, qt[HEAD_DIM:]], axis=0)
    ones = jnp.ones((ONES_ROWS, tk), BF16)
    m_ref[...] = jnp.full_like(m_ref, NEG)
    acc_ref[...] = jnp.zeros_like(acc_ref)

    n_ct = tq // ATT_COL_TILE

    def cols(ct):
        return slice(ct * ATT_COL_TILE, (ct + 1) * ATT_COL_TILE)

    def score_stage(j, slot, ct):
        k0 = pl.multiple_of(j * tk, tk)
        for a in range(2):
            s = jnp.dot(k_ref[pl.ds(k0, tk), a * LANES:(a + 1) * LANES],
                        wq_ref[a, :, cols(ct)], preferred_element_type=F32)
            s_refs[slot][a][:, cols(ct)] = s
            cmax_ref[slot, a, :, cols(ct)] = jnp.max(s, axis=0, keepdims=True)

    def softmax_pv_stage(j, slot, ct, masked):
        k0 = pl.multiple_of(j * tk, tk)
        for a in range(2):
            if masked:
                s = s_refs[slot][a][:, cols(ct)]
                key = k0 + lax.broadcasted_iota(jnp.int32, s.shape, 0)
                qry = (q0 + ct * ATT_COL_TILE) + lax.broadcasted_iota(jnp.int32, s.shape, 1)
                s = jnp.where(key <= qry, s, NEG)
                s_refs[slot][a][:, cols(ct)] = s
                col_max = jnp.max(s, axis=0, keepdims=True)
            else:
                col_max = cmax_ref[slot, a, :, cols(ct)]
            m_old = m_ref[a, :, cols(ct)]
            m_new = jnp.maximum(m_old, col_max)
            alpha = jnp.exp2(m_old - m_new)
            p = jnp.exp2(s_refs[slot][a][:, cols(ct)] - m_new).astype(BF16)
            m_ref[a, :, cols(ct)] = m_new
            v_aug = jnp.concatenate(
                [vt_ref[a * HEAD_DIM:(a + 1) * HEAD_DIM, pl.ds(k0, tk)], ones], axis=0)
            acc_ref[a, :, cols(ct)] = (
                alpha * acc_ref[a, :, cols(ct)]
                + jnp.dot(v_aug, p, preferred_element_type=F32))

    n_diag = tq // tk
    n_full = qi * n_diag

    def visibility(d, ct):
        if d < 0 or d >= n_diag:
            return "full" if d < 0 else "skip"
        first_key, last_key = d * tk, (d + 1) * tk - 1
        first_qry, last_qry = ct * ATT_COL_TILE, (ct + 1) * ATT_COL_TILE - 1
        if last_key <= first_qry:
            return "full"
        return "skip" if first_key > last_qry else "mask"

    def block_step(j, slot, d=-1):
        nxt = -1 if d < 0 else d + 1
        for ct in range(n_ct):
            if visibility(nxt, ct) != "skip":
                score_stage(j + 1, 1 - slot, ct)
            if visibility(d, ct) != "skip":
                softmax_pv_stage(j, slot, ct, visibility(d, ct) == "mask")

    for ct in range(n_ct):
        score_stage(0, 0, ct)

    def pair(ii, carry):
        block_step(2 * ii, 0)
        block_step(2 * ii + 1, 1)
        return carry

    lax.fori_loop(0, n_full // 2, pair, 0)
    for d in range(n_diag):
        block_step(n_full + d, d % 2, d)

    outs = []
    for a in range(2):
        acc = acc_ref[a]
        outs.append((acc[:HEAD_DIM] / acc[HEAD_DIM:HEAD_DIM + 1]).T)
    o_ref[...] = jnp.concatenate(outs, axis=1).astype(BF16)


def _fox_attn(q, k, v):
    bsz, seq, d = q.shape
    tq, tk = ATT_Q_TILE, ATT_K_TILE
    n_pairs = d // LANES
    return pl.pallas_call(
        _fox_attn_kernel,
        grid=(bsz, n_pairs, seq // tq),
        in_specs=[
            pl.BlockSpec((None, tq, LANES), lambda b, h, i: (b, i, h)),
            pl.BlockSpec((None, seq, 2 * LANES), lambda b, h, i: (b, 0, h)),
            pl.BlockSpec((None, seq, LANES), lambda b, h, i: (b, 0, h)),
        ],
        out_specs=pl.BlockSpec((None, tq, LANES), lambda b, h, i: (b, i, h)),
        out_shape=jax.ShapeDtypeStruct((bsz, seq, d), BF16),
        scratch_shapes=(
            [pltpu.VMEM((LANES, seq), BF16), pltpu.VMEM((2, LANES, tq), BF16)]
            + [pltpu.VMEM((tk, tq), F32)] * 4
            + [pltpu.VMEM((2, 2, 1, tq), F32), pltpu.VMEM((2, 1, tq), F32),
               pltpu.VMEM((2, HEAD_DIM + ONES_ROWS, tq), F32)]),
        compiler_params=_params(3),
        name="fox_attn",
    )(q, k, v)


def _head_gain(gain, scale):
    return (jnp.tile(gain.astype(F32), LANES // HEAD_DIM) * scale).reshape(1, LANES)


def kernel(x, c, ada_w, ada_b, norm_g, ffn_w13, ffn_w2, rel_bias, swa_w_in, swa_w_out,
           swa_q_g, swa_k_g, swa_sink, fox_w_in, fox_w_out, fox_b_f, fox_q_g, fox_k_g):
    bsz = x.shape[0]
    mod = _ada(c, ada_w, ada_b).reshape(DEPTH, bsz, 3, 3, 1, D_MODEL)
    q_scale = HEAD_DIM ** -0.5
    bias_tiles = _swa_bias_tiles(rel_bias.astype(F32) * LOG2E)

    w13_b = ffn_w13.astype(BF16)
    w2_b = ffn_w2.astype(BF16)

    for layer in range(DEPTH):
        g = norm_g[layer].reshape(3, 1, D_MODEL)
        m = mod[layer]
        x = _ffn(x, m[:, 0, 0], m[:, 0, 1], m[:, 0, 2], g[0], w13_b[layer, 0], w2_b[layer, 0])
        j = layer // 2
        mixer_out = None
        if layer % 2 == 0:
            q, k2, v2 = _swa_proj(x, m[:, 1, 0], m[:, 1, 1], g[1], swa_w_in[j].astype(BF16),
                                  _head_gain(swa_q_g[j], q_scale * LOG2E),
                                  _head_gain(swa_k_g[j], 1.0))
            sink = jnp.repeat((swa_sink[j].astype(F32) * LOG2E).reshape(A_KV_HEADS, A_GROUP), BLOCK,
                              axis=1).reshape(A_KV_HEADS, 1, A_GROUP * BLOCK)
            x = _swa_attn(sink, q, k2, v2, bias_tiles, x, m[:, 1, 2],
                          swa_w_out[j].astype(BF16))
        else:
            w_in = jnp.pad(fox_w_in[j], ((0, 0), (0, LANES - N_HEADS))).astype(BF16)
            b_f = jnp.pad(fox_b_f[j].astype(F32), (0, LANES - N_HEADS)).reshape(1, LANES)
            q, k, v = _fox_proj(x, m[:, 1, 0], m[:, 1, 1], g[1], w_in, b_f,
                                _head_gain(fox_q_g[j], q_scale * LOG2E),
                                _head_gain(fox_k_g[j], 1.0))
            mixer_out = (_fox_attn(q, k, v), m[:, 1, 2], fox_w_out[j].astype(BF16))
        x = _ffn(x, m[:, 2, 0], m[:, 2, 1], m[:, 2, 2], g[2], w13_b[layer, 1], w2_b[layer, 1],
                 mixer_out=mixer_out)
    return x
```

```python
import functools
import math

import numpy as np
import jax
import jax.numpy as jnp
from jax import lax
from jax.experimental import pallas as pl
from jax.experimental.pallas import tpu as pltpu

D_MODEL = 1024
DEPTH = 4
HEAD_DIM = 64
N_HEADS = D_MODEL // HEAD_DIM
A_KV_HEADS = 4
A_GROUP = N_HEADS // A_KV_HEADS
WINDOW = 128
BLOCK = 128
REL_BUCKETS = 32
REL_MAX_DIST = 128
D_FF = 2816
EPS = 1e-6

LANES = 128
FF_CHUNK = 256
N_FF_CHUNKS = D_FF // FF_CHUNK
ROW_TILE = 1024
ATT_Q_TILE = 2048
ATT_K_TILE = 512
ATT_COL_TILE = 256
ONES_ROWS = 16
FORGET_TERMS = 3
LOG2E = 1.4426950408889634
NEG = -1e30
VMEM_LIMIT = 56 * 1024 * 1024

BF16 = jnp.bfloat16
F32 = jnp.float32


def _params(n_axes):
    return pltpu.CompilerParams(
        dimension_semantics=("arbitrary",) * n_axes, vmem_limit_bytes=VMEM_LIMIT)


def _resident(shape, index_map):
    return pl.BlockSpec(shape, index_map, pipeline_mode=pl.Buffered(1))


def _modulated(x, g, shift, scale):
    ms = jnp.mean(x * x, axis=-1, keepdims=True)
    return (x * lax.rsqrt(ms + EPS) * g) * (1.0 + scale) + shift


def _ada_kernel(c_ref, w_ref, b_ref, o_ref):
    c = c_ref[...]
    c_act = c * jax.nn.sigmoid(c)
    o_ref[...] = jnp.dot(c_act, w_ref[...], precision=lax.Precision.HIGHEST,
                         preferred_element_type=F32) + b_ref[...]


def _ada(c, ada_w, ada_b):
    bsz = c.shape[0]
    n = ada_w.shape[-1]
    tn = D_MODEL
    return pl.pallas_call(
        _ada_kernel,
        grid=(DEPTH, n // tn),
        in_specs=[
            pl.BlockSpec((bsz, D_MODEL), lambda l, j: (0, 0)),
            pl.BlockSpec((None, D_MODEL, tn), lambda l, j: (l, 0, j)),
            pl.BlockSpec((None, 1, tn), lambda l, j: (l, 0, j)),
        ],
        out_specs=pl.BlockSpec((None, bsz, tn), lambda l, j: (l, 0, j)),
        out_shape=jax.ShapeDtypeStruct((DEPTH, bsz, n), F32),
        compiler_params=_params(2),
        name="ada_mod",
    )(c, ada_w, ada_b.reshape(DEPTH, 1, n))


def _ffn_kernel(*refs, mixer_out):
    if mixer_out:
        att_ref, mix_gate_ref, wo_ref = refs[:3]
        refs = refs[3:]
    x_ref, shift_ref, scale_ref, gate_ref, g_ref, w13_ref, w2_ref, o_ref, act_ref = refs
    x = x_ref[...]
    if mixer_out:
        x = x + mix_gate_ref[...] * jnp.dot(att_ref[...], wo_ref[...],
                                            preferred_element_type=F32)
    hb = _modulated(x, g_ref[...], shift_ref[...], scale_ref[...]).astype(BF16)
    for c in range(N_FF_CHUNKS):
        gate = jnp.dot(hb, w13_ref[:, c * FF_CHUNK:(c + 1) * FF_CHUNK],
                       preferred_element_type=F32)
        up = jnp.dot(hb, w13_ref[:, D_FF + c * FF_CHUNK:D_FF + (c + 1) * FF_CHUNK],
                     preferred_element_type=F32)
        act_ref[:, c * FF_CHUNK:(c + 1) * FF_CHUNK] = (
            gate * jax.nn.sigmoid(gate) * up).astype(BF16)
    y = jnp.dot(act_ref[...], w2_ref[...], preferred_element_type=F32)
    o_ref[...] = x + (0.5 * gate_ref[...]) * y


def _ffn(x, shift, scale, gate, g, w13, w2, mixer_out=None):
    bsz, seq, d = x.shape
    tm = ROW_TILE
    row = pl.BlockSpec((None, tm, d), lambda b, i: (b, i, 0))
    vec = pl.BlockSpec((None, 1, d), lambda b, i: (b, 0, 0))
    in_specs = [
        row, vec, vec, vec,
        pl.BlockSpec((1, d), lambda b, i: (0, 0)),
        _resident((d, 2 * D_FF), lambda b, i: (0, 0)),
        _resident((D_FF, d), lambda b, i: (0, 0)),
    ]
    args = (x, shift, scale, gate, g, w13, w2)
    if mixer_out is not None:
        in_specs = [row, vec, _resident((d, d), lambda b, i: (0, 0))] + in_specs
        args = tuple(mixer_out) + args
    return pl.pallas_call(
        functools.partial(_ffn_kernel, mixer_out=mixer_out is not None),
        grid=(bsz, seq // tm),
        in_specs=in_specs,
        out_specs=row,
        out_shape=jax.ShapeDtypeStruct(x.shape, F32),
        scratch_shapes=[pltpu.VMEM((tm, D_FF), BF16)],
        compiler_params=_params(2),
        name="ffn_half_step",
    )(*args)


def _head_norm(t, gain):
    lo = lax.broadcasted_iota(jnp.int32, t.shape, 1) < HEAD_DIM
    t2 = t * t
    ss_lo = jnp.sum(jnp.where(lo, t2, 0.0), axis=-1, keepdims=True)
    ss_hi = jnp.sum(jnp.where(lo, 0.0, t2), axis=-1, keepdims=True)
    rs = jnp.where(lo, lax.rsqrt(ss_lo * (1.0 / HEAD_DIM) + EPS),
                   lax.rsqrt(ss_hi * (1.0 / HEAD_DIM) + EPS))
    return t * rs * gain


def _proj_chunks(hb, w_ref, col0, n_chunks):
    for c in range(0, n_chunks, 2):
        width = min(2, n_chunks - c) * LANES
        t = jnp.dot(hb, w_ref[:, col0 + c * LANES:col0 + c * LANES + width],
                    preferred_element_type=F32)
        for i in range(width // LANES):
            yield c + i, t[:, i * LANES:(i + 1) * LANES]


def _swa_proj_kernel(x_ref, shift_ref, scale_ref, g_ref, w_ref, qg_ref, kg_ref,
                     q_ref, k_ref, v_ref):
    hb = _modulated(x_ref[...], g_ref[...], shift_ref[...], scale_ref[...]).astype(BF16)
    lo = lax.broadcasted_iota(jnp.int32, (hb.shape[0], LANES), 1) < HEAD_DIM
    for c, t in _proj_chunks(hb, w_ref, 0, D_MODEL // LANES):
        q_ref[:, c * LANES:(c + 1) * LANES] = _head_norm(t, qg_ref[...]).astype(BF16)
    kv_cols = A_KV_HEADS * HEAD_DIM

    def store_duplicated(ref, c, t):
        r = pltpu.roll(t, HEAD_DIM, axis=1)
        ref[:, (2 * c) * LANES:(2 * c + 1) * LANES] = jnp.where(lo, t, r).astype(BF16)
        ref[:, (2 * c + 1) * LANES:(2 * c + 2) * LANES] = jnp.where(lo, r, t).astype(BF16)

    for c, t in _proj_chunks(hb, w_ref, D_MODEL, kv_cols // LANES):
        store_duplicated(k_ref, c, _head_norm(t, kg_ref[...]))
    for c, t in _proj_chunks(hb, w_ref, D_MODEL + kv_cols, kv_cols // LANES):
        v_ref[:, c * LANES:(c + 1) * LANES] = t.astype(BF16)


def _swa_proj(x, shift, scale, g, w_in, qg, kg):
    bsz, seq, d = x.shape
    tm = ROW_TILE
    n_in = w_in.shape[1]
    kvw = A_KV_HEADS * LANES
    kv_cols = A_KV_HEADS * HEAD_DIM
    row = lambda n: pl.BlockSpec((None, tm, n), lambda b, i: (b, i, 0))
    vec = pl.BlockSpec((None, 1, d), lambda b, i: (b, 0, 0))
    const = lambda n: pl.BlockSpec((1, n), lambda b, i: (0, 0))
    return pl.pallas_call(
        _swa_proj_kernel,
        grid=(bsz, seq // tm),
        in_specs=[row(d), vec, vec, const(d),
                  _resident((d, n_in), lambda b, i: (0, 0)),
                  const(LANES), const(LANES)],
        out_specs=[row(d), row(kvw), row(kv_cols)],
        out_shape=[jax.ShapeDtypeStruct((bsz, seq, d), BF16),
                   jax.ShapeDtypeStruct((bsz, seq, kvw), BF16),
                   jax.ShapeDtypeStruct((bsz, seq, kv_cols), BF16)],
        compiler_params=_params(2),
        name="swa_proj",
    )(x, shift, scale, g, w_in, qg, kg)


def _swa_attn_kernel(sink_ref, q_ref, k_ref, v_ref, bias_ref, x_ref, gate_ref, wo_ref,
                     o_ref, att_ref):
    tq = q_ref.shape[0]
    q0 = pl.program_id(1) * tq
    zeros = jnp.zeros((HEAD_DIM, BLOCK), BF16)
    ones = jnp.ones((ONES_ROWS, 2 * BLOCK), BF16)
    units = [(sb, j) for sb in range(tq // BLOCK) for j in range(A_KV_HEADS)]

    def window(sb):
        row0 = q0 + sb * BLOCK
        return pl.multiple_of(jnp.maximum(row0 - BLOCK, 0), BLOCK), (row0 == 0).astype(jnp.int32)

    def score_stage(sb, j):
        kstart, first = window(sb)
        rows = slice(sb * BLOCK, (sb + 1) * BLOCK)
        kc = k_ref[pl.ds(kstart, 2 * BLOCK), j * LANES:(j + 1) * LANES]
        wq = []
        for half in range(2):
            qt = q_ref[rows, (2 * j + half) * LANES:(2 * j + half + 1) * LANES]
            qt = qt.astype(F32).T.astype(BF16)
            wq.append(jnp.concatenate([qt[:HEAD_DIM], zeros], axis=0))
            wq.append(jnp.concatenate([zeros, qt[HEAD_DIM:]], axis=0))
        s = jnp.dot(kc, jnp.concatenate(wq, axis=1), preferred_element_type=F32)
        return s + bias_ref[first, j]

    def softmax_stage(j, s):
        sink = sink_ref[j]
        m = jnp.maximum(jnp.max(s, axis=0, keepdims=True), sink)
        return jnp.exp2(s - m).astype(BF16), jnp.exp2(sink - m)

    def pv_stage(sb, j, p, p_sink):
        kstart, _ = window(sb)
        rows = slice(sb * BLOCK, (sb + 1) * BLOCK)
        vt = v_ref[pl.ds(kstart, 2 * BLOCK), (j // 2) * LANES:(j // 2 + 1) * LANES]
        vt = vt.astype(F32).T.astype(BF16)
        v_aug = jnp.concatenate(
            [vt[(j % 2) * HEAD_DIM:(j % 2 + 1) * HEAD_DIM], ones], axis=0)
        acc = jnp.dot(v_aug, p, preferred_element_type=F32)
        ot = acc[:HEAD_DIM] / (acc[HEAD_DIM:HEAD_DIM + 1] + p_sink)
        for half in range(2):
            pair = jnp.concatenate(
                [ot[:, (2 * half) * BLOCK:(2 * half + 1) * BLOCK],
                 ot[:, (2 * half + 1) * BLOCK:(2 * half + 2) * BLOCK]], axis=0)
            ch = 2 * j + half
            att_ref[rows, ch * LANES:(ch + 1) * LANES] = pair.T.astype(BF16)

    scores = score_stage(*units[0])
    pending = None
    for u, (sb, j) in enumerate(units):
        nxt = score_stage(*units[u + 1]) if u + 1 < len(units) else None
        probs = softmax_stage(j, scores)
        if pending is not None:
            pv_stage(*pending)
        pending = (sb, j) + probs
        scores = nxt
    pv_stage(*pending)
    y = jnp.dot(att_ref[...], wo_ref[...], preferred_element_type=F32)
    o_ref[...] = x_ref[...] + gate_ref[...] * y


def _swa_attn(sink, q, k2, v, bias, x, gate, w_out):
    bsz, seq, d = x.shape
    tq = ROW_TILE
    kvw = k2.shape[-1]
    row = lambda n: pl.BlockSpec((None, tq, n), lambda b, i: (b, i, 0))
    return pl.pallas_call(
        _swa_attn_kernel,
        grid=(bsz, seq // tq),
        in_specs=[
            pl.BlockSpec(sink.shape, lambda b, i: (0, 0, 0)),
            row(d),
            _resident((None, seq, kvw), lambda b, i: (b, 0, 0)),
            _resident((None, seq, v.shape[-1]), lambda b, i: (b, 0, 0)),
            _resident(bias.shape, lambda b, i: (0, 0, 0, 0)),
            row(d),
            pl.BlockSpec((None, 1, d), lambda b, i: (b, 0, 0)),
            _resident((d, d), lambda b, i: (0, 0)),
        ],
        out_specs=row(d),
        out_shape=jax.ShapeDtypeStruct(x.shape, F32),
        scratch_shapes=[pltpu.VMEM((tq, d), BF16)],
        compiler_params=_params(2),
        name="swa_attn_out",
    )(sink, q, k2, v, bias, x, gate, w_out)


def _rel_bucket_table():
    n = np.arange(WINDOW)
    max_exact = REL_BUCKETS // 2
    nf = np.maximum(n, 1).astype(np.float32)
    large = max_exact + (np.log(nf / max_exact) / math.log(REL_MAX_DIST / max_exact)
                         * (REL_BUCKETS - max_exact)).astype(np.int32)
    large = np.minimum(large, REL_BUCKETS - 1)
    return np.where(n < max_exact, n, large)


def _swa_bias_tiles(rel_bias):
    by_dist = rel_bias.astype(F32)[_rel_bucket_table()].T
    rev = by_dist[:, ::-1]
    n_h = rev.shape[0]
    span = 3 * BLOCK - 1
    tiles = []
    for key_offset in (BLOCK, 0):
        neg = lambda n: jnp.full((n_h, n), NEG, F32)
        w = jnp.concatenate([neg(key_offset), rev, neg(span - key_offset - WINDOW)], axis=1)
        z = jnp.concatenate([w[:, BLOCK - 1:], neg(1), w[:, :BLOCK - 1]], axis=1)
        length = z.shape[1]
        t = jnp.tile(z, (1, BLOCK))[:, :BLOCK * (length - 1)]
        t = t.reshape(n_h, BLOCK, length - 1)[:, :, :2 * BLOCK]
        b = t.reshape(A_KV_HEADS, A_GROUP * BLOCK, 2 * BLOCK)
        tiles.append(jnp.transpose(b, (0, 2, 1)))
    return jnp.stack(tiles)


def _split3(x):
    hi = x.astype(BF16).astype(F32)
    rest = x - hi
    mid = rest.astype(BF16).astype(F32)
    low = (rest - mid).astype(BF16).astype(F32)
    return (hi + pltpu.roll(mid, N_HEADS, axis=1)
            + pltpu.roll(low, 2 * N_HEADS, axis=1)).astype(BF16)


def _fox_proj_kernel(x_ref, shift_ref, scale_ref, g_ref, w_ref, bf_ref, qg_ref, kg_ref,
                     place_ref, q_ref, k_ref, v_ref, carry_ref):
    tm = x_ref.shape[0]
    hb = _modulated(x_ref[...], g_ref[...], shift_ref[...], scale_ref[...]).astype(BF16)
    n_c = D_MODEL // LANES
    for c, t in _proj_chunks(hb, w_ref, 0, n_c):
        q_ref[:, c * LANES:(c + 1) * LANES] = _head_norm(t, qg_ref[...]).astype(BF16)
    for c, t in _proj_chunks(hb, w_ref, 2 * D_MODEL, n_c):
        v_ref[:, c * LANES:(c + 1) * LANES] = t.astype(BF16)

    z = jnp.dot(hb, w_ref[:, 3 * D_MODEL:3 * D_MODEL + LANES],
                preferred_element_type=F32) + bf_ref[...]
    lane = lax.broadcasted_iota(jnp.int32, (tm, LANES), 1)
    live = lane < N_HEADS
    log_f = jnp.where(live, jnp.minimum(z, 0.0) - jnp.log1p(jnp.exp(-jnp.abs(z))), 0.0)
    r = lax.broadcasted_iota(jnp.int32, (tm, tm), 0)
    cidx = lax.broadcasted_iota(jnp.int32, (tm, tm), 1)
    tri = (cidx <= r).astype(BF16)
    part = jnp.dot(tri, _split3(log_f), preferred_element_type=F32)

    @pl.when(pl.program_id(1) == 0)
    def _():
        carry_ref[...] = jnp.zeros_like(carry_ref)

    cum = (part + pltpu.roll(part, LANES - N_HEADS, axis=1)
           + pltpu.roll(part, LANES - 2 * N_HEADS, axis=1)) + carry_ref[...]
    carry_ref[...] = cum[tm - 1:tm, :]

    aug = jnp.dot(_split3(jnp.where(live, cum * LOG2E, 0.0)), place_ref[...],
                  preferred_element_type=F32)
    lo = lane < HEAD_DIM
    for c, t in _proj_chunks(hb, w_ref, D_MODEL, n_c):
        t = _head_norm(t, kg_ref[...])
        even, odd = 2 * c, 2 * c + 1
        k_ref[:, even * LANES:(even + 1) * LANES] = jnp.where(
            lo, t, aug[:, even * LANES:(even + 1) * LANES]).astype(BF16)
        k_ref[:, odd * LANES:(odd + 1) * LANES] = jnp.where(
            lo, aug[:, odd * LANES:(odd + 1) * LANES], t).astype(BF16)


def _fox_proj(x, shift, scale, g, w_in, b_f, qg, kg):
    bsz, seq, d = x.shape
    tm = ROW_TILE
    n_in = w_in.shape[1]
    row = pl.BlockSpec((None, tm, d), lambda b, i: (b, i, 0))
    vec = pl.BlockSpec((None, 1, d), lambda b, i: (b, 0, 0))
    const = lambda n: pl.BlockSpec((1, n), lambda b, i: (0, 0))
    act = jax.ShapeDtypeStruct((bsz, seq, d), BF16)
    kw = N_HEADS * LANES
    place = np.zeros((LANES, kw), np.float32)
    for h in range(N_HEADS):
        for term in range(FORGET_TERMS):
            first = HEAD_DIM if h % 2 == 0 else 0
            place[term * N_HEADS + h, h * LANES + first + term] = 1.0
    return pl.pallas_call(
        _fox_proj_kernel,
        grid=(bsz, seq // tm),
        in_specs=[row, vec, vec, const(d),
                  _resident((d, n_in), lambda b, i: (0, 0)),
                  const(LANES), const(LANES), const(LANES),
                  _resident((LANES, kw), lambda b, i: (0, 0))],
        out_specs=[row, pl.BlockSpec((None, tm, kw), lambda b, i: (b, i, 0)), row],
        out_shape=[act, jax.ShapeDtypeStruct((bsz, seq, kw), BF16), act],
        scratch_shapes=[pltpu.VMEM((1, LANES), F32)],
        compiler_params=_params(2),
        name="fox_proj",
    )(x, shift, scale, g, w_in, b_f, qg, kg, jnp.asarray(place, BF16))


def _fox_attn_kernel(q_ref, k_ref, v_ref, o_ref, vt_ref, wq_ref,
                     s00_ref, s01_ref, s10_ref, s11_ref, cmax_ref, m_ref, acc_ref):
    s_refs = ((s00_ref, s01_ref), (s10_ref, s11_ref))
    tq = q_ref.shape[0]
    tk = ATT_K_TILE
    seq = k_ref.shape[0]
    qi = pl.program_id(2)

    @pl.when(qi == 0)
    def _():
        for c in range(seq // tk):
            blk = v_ref[c * tk:(c + 1) * tk, :].astype(F32)
            vt_ref[:, c * tk:(c + 1) * tk] = blk.T.astype(BF16)

    q0 = pl.multiple_of(qi * tq, tq)
    qt = q_ref[...].astype(F32).T.astype(BF16)
    row = lax.broadcasted_iota(jnp.int32, (HEAD_DIM, tq), 0)
    minus_ones = jnp.where(row < FORGET_TERMS, -1.0, 0.0).astype(BF16)
    wq_ref[0] = jnp.concatenate([qt[:HEAD_DIM], minus_ones], axis=0)
    wq_ref[1] = jnp.concatenate([minus_ones, qt[HEAD_DIM:]], axis=0)
    ones = jnp.ones((ONES_ROWS, tk), BF16)
    m_ref[...] = jnp.full_like(m_ref, NEG)
    acc_ref[...] = jnp.zeros_like(acc_ref)

    n_ct = tq // ATT_COL_TILE

    def cols(ct):
        return slice(ct * ATT_COL_TILE, (ct + 1) * ATT_COL_TILE)

    def score_stage(j, slot, ct):
        k0 = pl.multiple_of(j * tk, tk)
        for a in range(2):
            s = jnp.dot(k_ref[pl.ds(k0, tk), a * LANES:(a + 1) * LANES],
                        wq_ref[a, :, cols(ct)], preferred_element_type=F32)
            s_refs[slot][a][:, cols(ct)] = s
            cmax_ref[slot, a, :, cols(ct)] = jnp.max(s, axis=0, keepdims=True)

    def softmax_pv_stage(j, slot, ct, masked):
        k0 = pl.multiple_of(j * tk, tk)
        for a in range(2):
            if masked:
                s = s_refs[slot][a][:, cols(ct)]
                key = k0 + lax.broadcasted_iota(jnp.int32, s.shape, 0)
                qry = (q0 + ct * ATT_COL_TILE) + lax.broadcasted_iota(jnp.int32, s.shape, 1)
                s = jnp.where(key <= qry, s, NEG)
                s_refs[slot][a][:, cols(ct)] = s
                col_max = jnp.max(s, axis=0, keepdims=True)
            else:
                col_max = cmax_ref[slot, a, :, cols(ct)]
            m_old = m_ref[a, :, cols(ct)]
            m_new = jnp.maximum(m_old, col_max)
            alpha = jnp.exp2(m_old - m_new)
            p = jnp.exp2(s_refs[slot][a][:, cols(ct)] - m_new).astype(BF16)
            m_ref[a, :, cols(ct)] = m_new
            v_aug = jnp.concatenate(
                [vt_ref[a * HEAD_DIM:(a + 1) * HEAD_DIM, pl.ds(k0, tk)], ones], axis=0)
            acc_ref[a, :, cols(ct)] = (
                alpha * acc_ref[a, :, cols(ct)]
                + jnp.dot(v_aug, p, preferred_element_type=F32))

    n_diag = tq // tk
    n_full = qi * n_diag

    def visibility(d, ct):
        if d < 0 or d >= n_diag:
            return "full" if d < 0 else "skip"
        first_key, last_key = d * tk, (d + 1) * tk - 1
        first_qry, last_qry = ct * ATT_COL_TILE, (ct + 1) * ATT_COL_TILE - 1
        if last_key <= first_qry:
            return "full"
        return "skip" if first_key > last_qry else "mask"

    def block_step(j, slot, d=-1):
        nxt = -1 if d < 0 else d + 1
        for ct in range(n_ct):
            if visibility(nxt, ct) != "skip":
                score_stage(j + 1, 1 - slot, ct)
            if visibility(d, ct) != "skip":
                softmax_pv_stage(j, slot, ct, visibility(d, ct) == "mask")

    for ct in range(n_ct):
        score_stage(0, 0, ct)

    def pair(ii, carry):
        block_step(2 * ii, 0)
        block_step(2 * ii + 1, 1)
        return carry

    lax.fori_loop(0, n_full // 2, pair, 0)
    for d in range(n_diag):
        block_step(n_full + d, d % 2, d)

    outs = []
    for a in range(2):
        acc = acc_ref[a]
        outs.append((acc[:HEAD_DIM] / acc[HEAD_DIM:HEAD_DIM + 1]).T)
    o_ref[...] = jnp.concatenate(outs, axis=1).astype(BF16)


def _fox_attn(q, k, v):
    bsz, seq, d = q.shape
    tq, tk = ATT_Q_TILE, ATT_K_TILE
    n_pairs = d // LANES
    return pl.pallas_call(
        _fox_attn_kernel,
        grid=(bsz, n_pairs, seq // tq),
        in_specs=[
            pl.BlockSpec((None, tq, LANES), lambda b, h, i: (b, i, h)),
            pl.BlockSpec((None, seq, 2 * LANES), lambda b, h, i: (b, 0, h)),
            pl.BlockSpec((None, seq, LANES), lambda b, h, i: (b, 0, h)),
        ],
        out_specs=pl.BlockSpec((None, tq, LANES), lambda b, h, i: (b, i, h)),
        out_shape=jax.ShapeDtypeStruct((bsz, seq, d), BF16),
        scratch_shapes=(
            [pltpu.VMEM((LANES, seq), BF16), pltpu.VMEM((2, LANES, tq), BF16)]
            + [pltpu.VMEM((tk, tq), F32)] * 4
            + [pltpu.VMEM((2, 2, 1, tq), F32), pltpu.VMEM((2, 1, tq), F32),
               pltpu.VMEM((2, HEAD_DIM + ONES_ROWS, tq), F32)]),
        compiler_params=_params(3),
        name="fox_attn",
    )(q, k, v)


def _head_gain(gain, scale):
    return (jnp.tile(gain.astype(F32), LANES // HEAD_DIM) * scale).reshape(1, LANES)


def kernel(x, c, ada_w, ada_b, norm_g, ffn_w13, ffn_w2, rel_bias, swa_w_in, swa_w_out,
           swa_q_g, swa_k_g, swa_sink, fox_w_in, fox_w_out, fox_b_f, fox_q_g, fox_k_g):
    bsz = x.shape[0]
    mod = _ada(c, ada_w, ada_b).reshape(DEPTH, bsz, 3, 3, 1, D_MODEL)
    q_scale = HEAD_DIM ** -0.5
    bias_tiles = _swa_bias_tiles(rel_bias.astype(F32) * LOG2E)

    w13_b = ffn_w13.astype(BF16)
    w2_b = ffn_w2.astype(BF16)

    for layer in range(DEPTH):
        g = norm_g[layer].reshape(3, 1, D_MODEL)
        m = mod[layer]
        x = _ffn(x, m[:, 0, 0], m[:, 0, 1], m[:, 0, 2], g[0], w13_b[layer, 0], w2_b[layer, 0])
        j = layer // 2
        mixer_out = None
        if layer % 2 == 0:
            q, k2, v2 = _swa_proj(x, m[:, 1, 0], m[:, 1, 1], g[1], swa_w_in[j].astype(BF16),
                                  _head_gain(swa_q_g[j], q_scale * LOG2E),
                                  _head_gain(swa_k_g[j], 1.0))
            sink = jnp.repeat((swa_sink[j].astype(F32) * LOG2E).reshape(A_KV_HEADS, A_GROUP), BLOCK,
                              axis=1).reshape(A_KV_HEADS, 1, A_GROUP * BLOCK)
            x = _swa_attn(sink, q, k2, v2, bias_tiles, x, m[:, 1, 2],
                          swa_w_out[j].astype(BF16))
        else:
            w_in = jnp.pad(fox_w_in[j], ((0, 0), (0, LANES - N_HEADS))).astype(BF16)
            b_f = jnp.pad(fox_b_f[j].astype(F32), (0, LANES - N_HEADS)).reshape(1, LANES)
            q, k, v = _fox_proj(x, m[:, 1, 0], m[:, 1, 1], g[1], w_in, b_f,
                                _head_gain(fox_q_g[j], q_scale * LOG2E),
                                _head_gain(fox_k_g[j], 1.0))
            mixer_out = (_fox_attn(q, k, v), m[:, 1, 2], fox_w_out[j].astype(BF16))
        x = _ffn(x, m[:, 2, 0], m[:, 2, 1], m[:, 2, 2], g[2], w13_b[layer, 1], w2_b[layer, 1],
                 mixer_out=mixer_out)
    return x
```
